```python
import math
import jax, jax.numpy as jnp
from jax import lax
import numpy as np

D_MODEL = 1024
BATCH = 8
SEQ = 2048
DEPTH = 1

GRID_W = 64
CTX_LEN = 256

N_MOD = 6
NORM_EPS = 1e-6

MIX_WIDTH = D_MODEL
RET_HEADS = 4
RET_WIDTH = MIX_WIDTH // 2
RET_DV = RET_WIDTH // RET_HEADS
RET_DK = RET_DV // 2
RET_CHUNK = 128
DIFF_HEADS = 4
DIFF_WIDTH = MIX_WIDTH - RET_WIDTH
DIFF_DV = DIFF_WIDTH // DIFF_HEADS
DIFF_D = DIFF_DV // 2
Q_BLOCK = 128
ROPE_DIM = RET_DK
ROPE_BASE = 10000.0

RET_QK_COLS = RET_HEADS * RET_DK
IN_SPLITS = (RET_QK_COLS, 2 * RET_QK_COLS, 2 * RET_QK_COLS + RET_WIDTH, 2 * RET_QK_COLS + 2 * RET_WIDTH, 2 * RET_QK_COLS + 2 * RET_WIDTH + DIFF_WIDTH, 2 * RET_QK_COLS + 2 * RET_WIDTH + 2 * DIFF_WIDTH)
IN_COLS = 2 * RET_QK_COLS + 2 * RET_WIDTH + 3 * DIFF_WIDTH

PEER_HEADS = 8
PEER_NKEYS = 128
PEER_N_EXPERTS = PEER_NKEYS * PEER_NKEYS
PEER_QDIM = 256
PEER_HALF = PEER_QDIM // 2
PEER_TOPK = 16
PEER_BLOCK = 128

kernel_name = "hybrid_retention_diffattn_peer_dit_layer"


def rms_norm(x, g, eps=NORM_EPS):
    xf = x.astype(jnp.float32)
    y = xf * lax.rsqrt(jnp.mean(xf * xf, axis=-1, keepdims=True) + eps)
    return (y * g.astype(jnp.float32)).astype(x.dtype)


def modulate(x, g, shift, scale):
    return rms_norm(x, g) * (1.0 + scale) + shift


def adaln_params(cond, w_mod, b_mod):
    mod = jax.nn.silu(cond) @ w_mod + b_mod
    return mod.reshape(cond.shape[0], N_MOD, 1, cond.shape[-1])


def heads(t, n_heads):
    b, l, _ = t.shape
    return t.reshape(b, l, n_heads, -1).transpose(0, 2, 1, 3)


def merge_heads(t):
    b, h, l, e = t.shape
    return t.transpose(0, 2, 1, 3).reshape(b, l, h * e)


def axial_rope_tables(rows, head_dim, dtype):
    quarter = head_dim // 4
    freqs = ROPE_BASE ** (-jnp.arange(quarter, dtype=jnp.float32) / quarter)
    row = jnp.repeat(jnp.arange(rows, dtype=jnp.float32), GRID_W)
    col = jnp.tile(jnp.arange(GRID_W, dtype=jnp.float32), rows)
    ar = row[:, None] * freqs
    ac = col[:, None] * freqs
    ang = jnp.concatenate([ar, ar, ac, ac], axis=-1)
    return jnp.cos(ang).astype(dtype), jnp.sin(ang).astype(dtype)


def apply_rope(x, cos, sin):
    quarter = x.shape[-1] // 4
    xr = x.reshape(x.shape[:-1] + (2, 2, quarter))
    rot = jnp.concatenate([-xr[..., 1:, :], xr[..., :1, :]], axis=-2).reshape(x.shape)
    return x * cos + rot * sin


def decayed_state(k, v, log_gamma):
    l = k.shape[2]
    dist = (l - 1 - jnp.arange(l)).astype(jnp.float32)
    w = jnp.exp(dist[None, :] * log_gamma[:, None]).astype(k.dtype)
    return jnp.einsum('bhjd,hj,bhje->bhde', k, w, v)


def retention_chunkwise(q, k, v, log_gamma, init_state, strict):
    b, h, l, dk = q.shape
    dv = v.shape[-1]
    n = l // RET_CHUNK
    C = RET_CHUNK
    qc = q.reshape(b, h, n, C, dk)
    kc = k.reshape(b, h, n, C, dk)
    vc = v.reshape(b, h, n, C, dv)
    idx = jnp.arange(C, dtype=jnp.float32)
    diff = idx[:, None] - idx[None, :]
    mask = (diff > 0) if strict else (diff >= 0)
    lg = log_gamma[:, None, None]
    intra_decay = jnp.where(mask, jnp.exp(jnp.where(mask, diff, 0.0) * lg), 0.0).astype(q.dtype)
    scores = jnp.einsum('bhnid,bhnjd->bhnij', qc, kc) * intra_decay[None, :, None]
    out = jnp.einsum('bhnij,bhnje->bhnie', scores, vc)
    k_w = jnp.exp((C - 1 - idx)[None, :] * log_gamma[:, None]).astype(q.dtype)
    local = jnp.einsum('bhnjd,hj,bhnje->bhnde', kc, k_w, vc)
    chunk_decay = jnp.exp(C * log_gamma).astype(q.dtype)[None, :, None, None]

    def step(r, s):
        return chunk_decay * r + s, r

    _, r_prev = lax.scan(step, init_state.astype(local.dtype), jnp.moveaxis(local, 2, 0))
    r_prev = jnp.moveaxis(r_prev, 0, 2)
    q_w = jnp.exp((idx + 1.0)[None, :] * log_gamma[:, None]).astype(q.dtype)
    out = out + jnp.einsum('bhnid,hi,bhnde->bhnie', qc, q_w, r_prev)
    return out.reshape(b, h, l, dv)


def bidirectional_retention(q, k, v, lg_f, lg_b, state_f, state_b):
    flip = lambda t: jnp.flip(t, axis=2)
    y_f = retention_chunkwise(q, k, v, lg_f, state_f, strict=False)
    y_b = flip(retention_chunkwise(flip(q), flip(k), flip(v), lg_b, state_b, strict=True))
    return y_f + y_b


def retention_group(rq, rk, rv, rg, crq, crk, crv, crg, decay_logit, norm_g, cos, sin, with_ctx_out):
    scale = RET_DK ** -0.5
    log_gamma = jax.nn.log_sigmoid(decay_logit.astype(jnp.float32))
    lg_f, lg_b = log_gamma[0], log_gamma[1]
    q = apply_rope(heads(rq, RET_HEADS), cos, sin)
    k = apply_rope(heads(rk, RET_HEADS), cos, sin) * scale
    v = heads(rv, RET_HEADS)
    kc = heads(crk, RET_HEADS) * scale
    vc = heads(crv, RET_HEADS)
    state_f = decayed_state(kc, vc, lg_f)
    state_b = decayed_state(jnp.flip(kc, axis=2), jnp.flip(vc, axis=2), lg_b)
    y = bidirectional_retention(q, k, v, lg_f, lg_b, state_f, state_b)
    g_norm = norm_g[:, None, :]
    out = merge_heads(rms_norm(y, g_norm)) * jax.nn.silu(rg)
    ctx_out = None
    if with_ctx_out:
        qc = heads(crq, RET_HEADS)
        zero = jnp.zeros(state_f.shape, state_f.dtype)
        yc = bidirectional_retention(qc, kc, vc, lg_f, lg_b, zero, zero)
        ctx_out = merge_heads(rms_norm(yc, g_norm)) * jax.nn.silu(crg)
    return out, ctx_out


def diff_softmax_attend(q, k, v, lam):
    s = jnp.einsum('bhpqd,bhpkd->bhpqk', q, k).astype(jnp.float32) * (DIFF_D ** -0.5)
    p = jax.nn.softmax(s, axis=-1)
    a = (p[:, :, 0] - lam * p[:, :, 1]).astype(v.dtype)
    return jnp.einsum('bhqk,bhke->bhqe', a, v)


def diff_attention_group(dq, dk, dv, cdq, cdk, cdv, qk_norm_g, lam_params, norm_g, lam_init, cos, sin, with_ctx_out):
    def qk_heads(t, g):
        b_, l_, _ = t.shape
        t = t.reshape(b_, l_, DIFF_HEADS, 2, DIFF_D).transpose(0, 2, 3, 1, 4)
        return rms_norm(t, g)

    q = apply_rope(qk_heads(dq, qk_norm_g[0]), cos, sin)
    k = apply_rope(qk_heads(dk, qk_norm_g[1]), cos, sin)
    v = heads(dv, DIFF_HEADS)
    kc = qk_heads(cdk, qk_norm_g[1])
    vc = heads(cdv, DIFF_HEADS)
    lp = lam_params.astype(jnp.float32)
    lam = jnp.exp(jnp.sum(lp[0] * lp[1])) - jnp.exp(jnp.sum(lp[2] * lp[3])) + lam_init
    keys = jnp.concatenate([kc, k], axis=3)
    vals = jnp.concatenate([vc, v], axis=2)
    b, h, _, l, d = q.shape
    nb = l // Q_BLOCK
    q_blocks = jnp.moveaxis(q.reshape(b, h, 2, nb, Q_BLOCK, d), 3, 0)
    y = lax.map(lambda qb: diff_softmax_attend(qb, keys, vals, lam), q_blocks)
    y = jnp.moveaxis(y, 0, 2).reshape(b, h, l, DIFF_DV)
    g_norm = norm_g[:, None, :]
    out = merge_heads(rms_norm(y, g_norm) * (1.0 - lam_init))
    ctx_out = None
    if with_ctx_out:
        qc = qk_heads(cdq, qk_norm_g[0])
        yc = diff_softmax_attend(qc, kc, vc, lam)
        ctx_out = merge_heads(rms_norm(yc, g_norm) * (1.0 - lam_init))
    return out, ctx_out


def peer_ffn(h, w_query, sub_keys, expert_u, expert_v):
    b, l, d = h.shape
    tokens = h.reshape(-1, PEER_BLOCK, d)

    def block_fn(hb):
        t = hb.shape[0]
        q = (hb @ w_query).reshape(t, PEER_HEADS, 2, PEER_HALF)
        s = jnp.einsum('thpc,hpnc->thpn', q, sub_keys).astype(jnp.float32)
        v1, i1 = lax.top_k(s[:, :, 0], PEER_TOPK)
        v2, i2 = lax.top_k(s[:, :, 1], PEER_TOPK)
        cand = (v1[..., :, None] + v2[..., None, :]).reshape(t, PEER_HEADS, PEER_TOPK * PEER_TOPK)
        cand_idx = (i1[..., :, None] * PEER_NKEYS + i2[..., None, :]).reshape(t, PEER_HEADS, PEER_TOPK * PEER_TOPK)
        best, pos = lax.top_k(cand, PEER_TOPK)
        experts = jnp.take_along_axis(cand_idx, pos, axis=-1)
        gate = jax.nn.softmax(best, axis=-1).astype(hb.dtype)
        u = expert_u[experts]
        vsel = expert_v[experts]
        act = jax.nn.gelu(jnp.einsum('thkd,td->thk', u, hb), approximate=False)
        return jnp.einsum('thk,thkd->td', gate * act, vsel)

    return lax.map(block_fn, tokens).reshape(b, l, d)


def setup_inputs(seed: int = 0) -> dict:
    key = jax.random.key(seed)
    ks = jax.random.split(key, 20)
    nrm = lambda k, shape, s: jax.random.normal(k, shape, jnp.float32) * s
    L = DEPTH
    D = D_MODEL
    decay_init = jnp.asarray(np.log(2.0 ** (5 + np.arange(RET_HEADS)) - 1.0).astype(np.float32))
    return {
        "x": nrm(ks[0], (BATCH, SEQ, D), 1.0),
        "c": nrm(ks[1], (BATCH, D), 1.0),
        "ctx": nrm(ks[2], (BATCH, CTX_LEN, D), 1.0),
        "c_ctx": nrm(ks[3], (D,), 1.0),
        "w_mod": nrm(ks[4], (L, D, N_MOD * D), 0.5 * D ** -0.5),
        "b_mod": nrm(ks[5], (L, N_MOD * D), 0.02),
        "norm1_g": 1.0 + nrm(ks[6], (L, D), 0.02),
        "norm2_g": 1.0 + nrm(ks[7], (L, D), 0.02),
        "w_in": nrm(ks[8], (L, D, IN_COLS), D ** -0.5),
        "ret_decay_logit": decay_init[None, None, :] + nrm(ks[9], (L, 2, RET_HEADS), 0.1),
        "ret_norm_g": 1.0 + nrm(ks[10], (L, RET_HEADS, RET_DV), 0.02),
        "diff_qk_norm_g": 1.0 + nrm(ks[11], (L, 2, DIFF_D), 0.02),
        "diff_lambda": nrm(ks[12], (L, 4, DIFF_D), 0.1),
        "diff_norm_g": 1.0 + nrm(ks[13], (L, DIFF_HEADS, DIFF_DV), 0.02),
        "w_out": nrm(ks[14], (L, MIX_WIDTH, D), MIX_WIDTH ** -0.5),
        "peer_w_query": nrm(ks[15], (L, D, PEER_HEADS * PEER_QDIM), D ** -0.5),
        "peer_sub_keys": nrm(ks[16], (L, PEER_HEADS, 2, PEER_NKEYS, PEER_HALF), PEER_HALF ** -0.5),
        "peer_u": nrm(ks[17], (L, PEER_N_EXPERTS, D), D ** -0.5),
        "peer_v": nrm(ks[18], (L, PEER_N_EXPERTS, D), (PEER_HEADS * PEER_TOPK) ** -0.5),
    }


def reference(x, c, ctx, c_ctx, w_mod, b_mod, norm1_g, norm2_g, w_in, ret_decay_logit, ret_norm_g, diff_qk_norm_g, diff_lambda, diff_norm_g, w_out, peer_w_query, peer_sub_keys, peer_u, peer_v):
    rows = x.shape[1] // GRID_W
    cos, sin = axial_rope_tables(rows, ROPE_DIM, x.dtype)
    for layer in range(DEPTH):
        last = layer == DEPTH - 1
        lam_init = 0.8 - 0.6 * math.exp(-0.3 * layer)
        mod = adaln_params(c, w_mod[layer], b_mod[layer])
        mod_c = adaln_params(c_ctx[None, :], w_mod[layer], b_mod[layer])
        h = modulate(x, norm1_g[layer], mod[:, 0], mod[:, 1])
        hc = modulate(ctx, norm1_g[layer], mod_c[:, 0], mod_c[:, 1])
        rq, rk, rv, rg, dq, dk, dv = jnp.split(h @ w_in[layer], IN_SPLITS, axis=-1)
        crq, crk, crv, crg, cdq, cdk, cdv = jnp.split(hc @ w_in[layer], IN_SPLITS, axis=-1)
        ret_lat, ret_ctx = retention_group(rq, rk, rv, rg, crq, crk, crv, crg, ret_decay_logit[layer], ret_norm_g[layer], cos, sin, not last)
        diff_lat, diff_ctx = diff_attention_group(dq, dk, dv, cdq, cdk, cdv, diff_qk_norm_g[layer], diff_lambda[layer], diff_norm_g[layer], lam_init, cos, sin, not last)
        x = x + mod[:, 2] * (jnp.concatenate([ret_lat, diff_lat], axis=-1) @ w_out[layer])
        h2 = modulate(x, norm2_g[layer], mod[:, 3], mod[:, 4])
        x = x + mod[:, 5] * peer_ffn(h2, peer_w_query[layer], peer_sub_keys[layer], peer_u[layer], peer_v[layer])
        if not last:
            ctx = ctx + mod_c[:, 2] * (jnp.concatenate([ret_ctx, diff_ctx], axis=-1) @ w_out[layer])
            hc2 = modulate(ctx, norm2_g[layer], mod_c[:, 3], mod_c[:, 4])
            ctx = ctx + mod_c[:, 5] * peer_ffn(hc2, peer_w_query[layer], peer_sub_keys[layer], peer_u[layer], peer_v[layer])
    return x
```

```python
import functools
import math

import jax
import jax.numpy as jnp
from jax import lax
from jax.experimental import pallas as pl
from jax.experimental.pallas import tpu as pltpu

F32 = jnp.float32
BF16 = jnp.bfloat16

D_MODEL_ = 1024
N_MOD_ = 6
NORM_EPS_ = 1e-6
RET_HEADS_ = 4
RET_DK_ = 64
RET_DV_ = 128
RET_CHUNK_ = 128
DIFF_HEADS_ = 4
DIFF_D_ = 64
DIFF_DV_ = 128
GRID_W_ = 64
ROPE_BASE_ = 10000.0
PEER_HEADS_ = 8
PEER_NKEYS_ = 128
PEER_HALF_ = 128
PEER_TOPK_ = 16
LAM_INIT_ = 0.8 - 0.6 * math.exp(-0.3 * 0)

LANES_ = 128
SUBLANES_ = 8
VMEM_LIMIT_BYTES_ = 56 * 1024 * 1024

RET_QK_ = RET_HEADS_ * RET_DK_
RET_W_ = RET_HEADS_ * RET_DV_
DIFF_W_ = DIFF_HEADS_ * DIFF_DV_
IN_COLS_ = 2 * RET_QK_ + 2 * RET_W_ + 3 * DIFF_W_


def _dot(a, b):
    return jnp.dot(a, b, preferred_element_type=F32)


def _dot_nt(a, b):
    return lax.dot_general(a, b, (((1,), (1,)), ((), ())), preferred_element_type=F32)


def _sigmoid(x):
    return 1.0 / (1.0 + jnp.exp(-x))


def _params(sem):
    return pltpu.CompilerParams(dimension_semantics=sem, vmem_limit_bytes=VMEM_LIMIT_BYTES_)


def _adaln_kernel(cond_ref, w_ref, b_ref, o_ref):
    a = cond_ref[...]
    a = a * _sigmoid(a)
    o_ref[...] = _dot(a.astype(BF16), w_ref[...].astype(BF16)) + b_ref[...]


def _adaln(cond, w_mod, b_mod):
    rows, d = cond.shape
    n = w_mod.shape[1]
    tn = d
    return pl.pallas_call(
        _adaln_kernel,
        grid=(n // tn,),
        in_specs=[
            pl.BlockSpec((rows, d), lambda j: (0, 0)),
            pl.BlockSpec((d, tn), lambda j: (0, j)),
            pl.BlockSpec((1, tn), lambda j: (0, j)),
        ],
        out_specs=pl.BlockSpec((rows, tn), lambda j: (0, j)),
        out_shape=jax.ShapeDtypeStruct((rows, n), F32),
        compiler_params=_params(("parallel",)),
        name="adaln",
    )(cond, w_mod, b_mod.reshape(1, n))


def _rope_slab(v, cos, sin_up, sin_dn):
    up = pltpu.roll(v, LANES_ - 16, 1)
    dn = pltpu.roll(v, 16, 1)
    return v * cos + up * sin_up + dn * sin_dn


def _group_rms_scale(v, eps):
    w = v.shape[-1]
    sel = (lax.broadcasted_iota(jnp.int32, (w, LANES_), 0) // DIFF_D_
           == lax.broadcasted_iota(jnp.int32, (w, LANES_), 1)).astype(BF16)
    expand = (lax.broadcasted_iota(jnp.int32, (LANES_, w), 1) // DIFF_D_
              == lax.broadcasted_iota(jnp.int32, (LANES_, w), 0)).astype(BF16)
    v2 = v * v
    hi = v2.astype(BF16)
    lo = (v2 - hi.astype(F32)).astype(BF16)
    ssum = _dot(hi, sel) + _dot(lo, sel)
    r = lax.rsqrt(ssum * (1.0 / DIFF_D_) + eps)
    rhi = r.astype(BF16)
    rlo = (r - rhi.astype(F32)).astype(BF16)
    return _dot(rhi, expand) + _dot(rlo, expand)


def _inproj_kernel(x_ref, sh_ref, sc_ref, g_ref, w_ref, cos_ref, su_ref, sd_ref, qkg_ref,
                   rq_ref, rk_ref, rv_ref, rg_ref, dq_ref, dk_ref, dv_ref, y_scr, *, rope):
    x = x_ref[0]
    ms = jnp.mean(x * x, axis=-1, keepdims=True)
    h = x * lax.rsqrt(ms + NORM_EPS_) * g_ref[...]
    h = h * (1.0 + sc_ref[0]) + sh_ref[0]
    y_scr[...] = _dot(h.astype(BF16), w_ref[...])

    cos = cos_ref[...]
    su = su_ref[...]
    sd = sd_ref[...]

    def put(dst_ref, src_col, width, scale, norm_g):
        v = y_scr[:, src_col:src_col + width]
        if norm_g is not None:
            v = v * _group_rms_scale(v, NORM_EPS_) * norm_g
        for s in range(width // LANES_):
            slab = v[:, s * LANES_:(s + 1) * LANES_]
            if rope:
                slab = _rope_slab(slab, cos, su, sd)
            if scale != 1.0:
                slab = slab * scale
            dst_ref[0, :, s * LANES_:(s + 1) * LANES_] = slab.astype(dst_ref.dtype)

    c = 0
    put(rq_ref, c, RET_QK_, 1.0, None)
    c += RET_QK_
    put(rk_ref, c, RET_QK_, RET_DK_ ** -0.5, None)
    c += RET_QK_
    rv_ref[0] = y_scr[:, c:c + RET_W_].astype(BF16)
    c += RET_W_
    rg_ref[0] = y_scr[:, c:c + RET_W_]
    c += RET_W_
    put(dq_ref, c, DIFF_W_, 1.0, qkg_ref[0:1, :])
    c += DIFF_W_
    put(dk_ref, c, DIFF_W_, 1.0, qkg_ref[1:2, :])
    c += DIFF_W_
    dv_ref[0] = y_scr[:, c:c + DIFF_W_].astype(BF16)


def _inproj(xs, mod3, mod_row, norm_g, w_in_bf, tabs, qkg, rope, tm):
    b, l, d = xs.shape
    cos, su, sd = tabs
    grid = (b, l // tm)
    tok = lambda w: pl.BlockSpec((1, tm, w), lambda i, j: (i, j, 0))
    tab = pl.BlockSpec((tm, LANES_), lambda i, j: (j, 0))
    full = lambda a: pl.BlockSpec(a.shape, lambda i, j: (0,) * a.ndim)
    out_shapes = (
        jax.ShapeDtypeStruct((b, l, RET_QK_), BF16),
        jax.ShapeDtypeStruct((b, l, RET_QK_), BF16),
        jax.ShapeDtypeStruct((b, l, RET_W_), BF16),
        jax.ShapeDtypeStruct((b, l, RET_W_), F32),
        jax.ShapeDtypeStruct((b, l, DIFF_W_), BF16),
        jax.ShapeDtypeStruct((b, l, DIFF_W_), BF16),
        jax.ShapeDtypeStruct((b, l, DIFF_W_), BF16),
    )
    return pl.pallas_call(
        functools.partial(_inproj_kernel, rope=rope),
        grid=grid,
        in_specs=[
            tok(d),
            pl.BlockSpec((1, 1, d), lambda i, j: (mod_row(i), 0, 0)),
            pl.BlockSpec((1, 1, d), lambda i, j: (mod_row(i), 0, 1)),
            full(norm_g),
            full(w_in_bf),
            tab, tab, tab,
            full(qkg),
        ],
        out_specs=(tok(RET_QK_), tok(RET_QK_), tok(RET_W_), tok(RET_W_),
                   tok(DIFF_W_), tok(DIFF_W_), tok(DIFF_W_)),
        out_shape=out_shapes,
        scratch_shapes=[pltpu.VMEM((tm, IN_COLS_), F32)],
        compiler_params=_params(("parallel", "parallel")),
        name="inproj_rope" if rope else "inproj_ctx",
    )(xs, mod3, mod3, norm_g, w_in_bf, cos, su, sd, qkg)


def _log_sigmoid(x):
    return jnp.minimum(x, 0.0) - jnp.log(1.0 + jnp.exp(-jnp.abs(x)))


def _retention_kernel(dl_ref, q_ref, k_ref, v_ref, rg_ref, ck_ref, cv_ref, g_ref, o_ref,
                      lf_scr, lb_scr, rf_scr, rb_scr, *, seq, ctx_len):
    c = RET_CHUNK_
    n_chunks = seq // c
    head = pl.program_id(1)
    lane = lax.broadcasted_iota(jnp.int32, (1, LANES_), 1)
    head_lanes = (lane // RET_DK_) == (head % 2)

    dl = dl_ref[0]
    lg_f = _log_sigmoid(dl[0:1, :])
    lg_b = _log_sigmoid(dl[1:2, :])

    row = lax.broadcasted_iota(jnp.int32, (c, LANES_), 0).astype(F32)
    col = lax.broadcasted_iota(jnp.int32, (c, LANES_), 1).astype(F32)
    kw_f = jnp.exp((c - 1.0 - row) * lg_f)
    kw_b = jnp.exp(row * lg_b)
    qw_f = jnp.exp((row + 1.0) * lg_f)
    qw_b = jnp.exp((c - row) * lg_b)
    dist = row - col
    decay = jnp.where(dist >= 0.0,
                      jnp.exp(jnp.maximum(dist, 0.0) * lg_f),
                      jnp.exp(jnp.maximum(-dist, 0.0) * lg_b))
    chunk_f = jnp.exp(float(c) * lg_f)
    chunk_b = jnp.exp(float(c) * lg_b)

    crow = lax.broadcasted_iota(jnp.int32, (ctx_len, LANES_), 0).astype(F32)
    ck = ck_ref[0].astype(F32)
    cv = cv_ref[0]
    s_f = _dot((ck * jnp.exp((ctx_len - 1.0 - crow) * lg_f)).T.astype(BF16), cv)
    s_b = _dot((ck * jnp.exp(crow * lg_b)).T.astype(BF16), cv)

    def local_states(n, carry):
        sl = pl.ds(pl.multiple_of(n * c, c), c)
        kn = k_ref[0, sl, :].astype(F32)
        vn = v_ref[0, sl, :]
        lf_scr[n] = _dot((kn * kw_f).T.astype(BF16), vn)
        lb_scr[n] = _dot((kn * kw_b).T.astype(BF16), vn)
        return carry

    lax.fori_loop(0, n_chunks, local_states, 0)

    def scan_f(n, r):
        rf_scr[n] = r
        return chunk_f * r + lf_scr[n]

    lax.fori_loop(0, n_chunks, scan_f, s_f)

    def scan_b(m, r):
        n = n_chunks - 1 - m
        rb_scr[n] = r
        return chunk_b * r + lb_scr[n]

    lax.fori_loop(0, n_chunks, scan_b, s_b)

    gain = g_ref[0]

    def outputs(n, carry):
        sl = pl.ds(pl.multiple_of(n * c, c), c)
        qn = jnp.where(head_lanes, q_ref[0, sl, :], jnp.zeros((), BF16))
        scores = _dot_nt(qn, k_ref[0, sl, :])
        y = _dot((scores * decay).astype(BF16), v_ref[0, sl, :])
        y = y + qw_f * _dot(qn, rf_scr[n].astype(BF16)) + qw_b * _dot(qn, rb_scr[n].astype(BF16))
        ms = jnp.mean(y * y, axis=-1, keepdims=True)
        yn = y * lax.rsqrt(ms + NORM_EPS_) * gain
        rg = rg_ref[0, sl, :]
        o_ref[0, sl, :] = (yn * (rg * _sigmoid(rg))).astype(o_ref.dtype)
        return carry

    lax.fori_loop(0, n_chunks, outputs, 0)


def _retention(rq, rk, rv, rg, crk, crv, decay_b, norm_g3):
    b, l, _ = rq.shape
    ctx_len = crk.shape[1]
    n_chunks = l // RET_CHUNK_
    pair = lambda length: pl.BlockSpec((1, length, LANES_), lambda i, h: (i, 0, h // 2))
    own = lambda length: pl.BlockSpec((1, length, LANES_), lambda i, h: (i, 0, h))
    state = pltpu.VMEM((n_chunks, LANES_, RET_DV_), F32)
    return pl.pallas_call(
        functools.partial(_retention_kernel, seq=l, ctx_len=ctx_len),
        grid=(b, RET_HEADS_),
        in_specs=[
            pl.BlockSpec((1, 2, LANES_), lambda i, h: (h, 0, 0)),
            pair(l), pair(l), own(l), own(l), pair(ctx_len), own(ctx_len),
            pl.BlockSpec((1, 1, RET_DV_), lambda i, h: (h, 0, 0)),
        ],
        out_specs=own(l),
        out_shape=jax.ShapeDtypeStruct((b, l, RET_W_), BF16),
        scratch_shapes=[state, state, state, state],
        compiler_params=_params(("parallel", "parallel")),
        name="retention",
    )(decay_b, rq, rk, rv, rg, crk, crv, norm_g3)


def _diffattn_kernel(lam_ref, q_ref, k_ref, v_ref, ck_ref, cv_ref, g_ref, o_ref, *, seq, tk):
    q = q_ref[0]
    tq = q.shape[0]
    lane = lax.broadcasted_iota(jnp.int32, (1, LANES_), 1)
    first = lane < DIFF_D_
    zero = jnp.zeros((), BF16)
    q1 = jnp.where(first, q, zero)
    q2 = jnp.where(first, zero, q)
    scale = DIFF_D_ ** -0.5

    lp = lam_ref[...]
    lam = (jnp.exp(jnp.sum(lp[0:1] * lp[1:2], axis=-1, keepdims=True))
           - jnp.exp(jnp.sum(lp[2:3] * lp[3:4], axis=-1, keepdims=True)) + LAM_INIT_)

    def update(qh, kb, vb, m, l, acc):
        s = _dot_nt(qh, kb) * scale
        m_new = jnp.maximum(m, jnp.max(s, axis=-1, keepdims=True))
        alpha = jnp.exp(m - m_new)
        p = jnp.exp(s - m_new)
        l_new = alpha * l + jnp.sum(p, axis=-1, keepdims=True)
        acc_new = alpha * acc + _dot(p.astype(BF16), vb)
        return m_new, l_new, acc_new

    def step(kb, vb, carry):
        m1, l1, a1, m2, l2, a2 = carry
        m1, l1, a1 = update(q1, kb, vb, m1, l1, a1)
        m2, l2, a2 = update(q2, kb, vb, m2, l2, a2)
        return m1, l1, a1, m2, l2, a2

    neg = jnp.full((tq, 1), -jnp.inf, F32)
    zl = jnp.zeros((tq, 1), F32)
    za = jnp.zeros((tq, DIFF_DV_), F32)
    carry = step(ck_ref[0], cv_ref[0], (neg, zl, za, neg, zl, za))

    def body(j, carry):
        sl = pl.ds(pl.multiple_of(j * tk, tk), tk)
        return step(k_ref[0, sl, :], v_ref[0, sl, :], carry)

    m1, l1, a1, m2, l2, a2 = lax.fori_loop(0, seq // tk, body, carry)
    y = a1 / l1 - lam * (a2 / l2)
    ms = jnp.mean(y * y, axis=-1, keepdims=True)
    yn = y * lax.rsqrt(ms + NORM_EPS_) * g_ref[0]
    o_ref[0] = (yn * (1.0 - LAM_INIT_)).astype(o_ref.dtype)


def _diffattn(dq, dk, dv, cdk, cdv, lam_params, norm_g3, tq, tk):
    b, l, _ = dq.shape
    ctx_len = cdk.shape[1]
    kv = lambda length: pl.BlockSpec((1, length, LANES_), lambda i, h, j: (i, 0, h))
    qo = pl.BlockSpec((1, tq, LANES_), lambda i, h, j: (i, j, h))
    return pl.pallas_call(
        functools.partial(_diffattn_kernel, seq=l, tk=tk),
        grid=(b, DIFF_HEADS_, l // tq),
        in_specs=[
            pl.BlockSpec(lam_params.shape, lambda i, h, j: (0, 0)),
            qo, kv(l), kv(l), kv(ctx_len), kv(ctx_len),
            pl.BlockSpec((1, 1, DIFF_DV_), lambda i, h, j: (h, 0, 0)),
        ],
        out_specs=qo,
        out_shape=jax.ShapeDtypeStruct((b, l, DIFF_W_), BF16),
        compiler_params=_params(("parallel", "parallel", "parallel")),
        name="diffattn",
    )(lam_params, dq, dk, dv, cdk, cdv, norm_g3)


def _mixout_kernel(ret_ref, dif_ref, x_ref, gate_ref, sh_ref, sc_ref, g_ref, woa_ref, wob_ref,
                   wqt_ref, keys_ref, x1_ref, ht_ref, st_ref):
    o = _dot(ret_ref[0], woa_ref[...]) + _dot(dif_ref[0], wob_ref[...])
    x1 = x_ref[0] + gate_ref[0] * o
    x1_ref[0] = x1
    ms = jnp.mean(x1 * x1, axis=-1, keepdims=True)
    h2 = x1 * lax.rsqrt(ms + NORM_EPS_) * g_ref[...]
    h2 = h2 * (1.0 + sc_ref[0]) + sh_ref[0]
    ht = h2.T.astype(BF16)
    ht_ref[...] = ht
    qt = _dot(wqt_ref[...], ht).astype(BF16)
    for hp in range(2 * PEER_HEADS_):
        st_ref[hp] = _dot(keys_ref[hp], qt[hp * PEER_HALF_:(hp + 1) * PEER_HALF_, :])


def _mixout(ret, dif, x, mod3, norm_g, wo_a, wo_b, wq_t, keys, tm):
    b, l, d = x.shape
    t = b * l
    per = l // tm
    tok = lambda w: pl.BlockSpec((1, tm, w), lambda i, j: (i, j, 0))
    modrow = lambda k: pl.BlockSpec((1, 1, d), lambda i, j: (i, 0, k))
    full = lambda a: pl.BlockSpec(a.shape, lambda i, j: (0,) * a.ndim)
    n_hp = 2 * PEER_HEADS_
    return pl.pallas_call(
        _mixout_kernel,
        grid=(b, per),
        in_specs=[tok(RET_W_), tok(DIFF_W_), tok(d), modrow(2), modrow(3), modrow(4),
                  full(norm_g), full(wo_a), full(wo_b), full(wq_t), full(keys)],
        out_specs=(
            tok(d),
            pl.BlockSpec((d, tm), lambda i, j: (0, i * per + j)),
            pl.BlockSpec((n_hp, PEER_NKEYS_, tm), lambda i, j: (0, 0, i * per + j)),
        ),
        out_shape=(
            jax.ShapeDtypeStruct((b, l, d), F32),
            jax.ShapeDtypeStruct((d, t), BF16),
            jax.ShapeDtypeStruct((n_hp, PEER_NKEYS_, t), F32),
        ),
        compiler_params=_params(("parallel", "parallel")),
        name="mix_out",
    )(ret, dif, x, mod3, mod3, mod3, norm_g, wo_a, wo_b, wq_t, keys)


def _sort16_pairs():
    def merge(lo, hi, r):
        step = r * 2
        if step < hi - lo:
            yield from merge(lo, hi, step)
            yield from merge(lo + r, hi, step)
            for i in range(lo + r, hi - r, step):
                yield (i, i + r)
        else:
            yield (lo, lo + r)

    def sort(lo, hi):
        if hi - lo >= 1:
            mid = lo + (hi - lo) // 2
            yield from sort(lo, mid)
            yield from sort(mid + 1, hi)
            yield from merge(lo, hi, 1)

    return tuple(sort(0, 15))


SORT16_PAIRS_ = _sort16_pairs()


def _bitonic_merge_desc(c):
    c = list(c)
    d = len(c) // 2
    while d >= 1:
        for k in range(len(c)):
            if k & d == 0:
                hi = jnp.maximum(c[k], c[k + d])
                lo = jnp.minimum(c[k], c[k + d])
                c[k], c[k + d] = hi, lo
        d //= 2
    return c


def _merge_top(a, b):
    n = len(a)
    return _bitonic_merge_desc([jnp.maximum(a[k], b[n - 1 - k]) for k in range(n)])


def _merge_sublanes(a):
    for shift in (4, 2, 1):
        a = _merge_top(a, [pltpu.roll(v, shift, 0) for v in a])
    return a


def _top16_rows(x_ref, tr):
    n = PEER_NKEYS_ // SUBLANES_
    a = [x_ref[v * SUBLANES_:(v + 1) * SUBLANES_, :] for v in range(n)]
    for i, j in SORT16_PAIRS_:
        hi = jnp.maximum(a[i], a[j])
        lo = jnp.minimum(a[i], a[j])
        a[i], a[j] = hi, lo
    return _merge_sublanes(a)


def _route_kernel(s_ref, th_ref, c1_ref, e2_ref):
    tr = s_ref.shape[-1]
    k = PEER_TOPK_
    s1_ref = s_ref.at[0, 0]
    s2_ref = s_ref.at[0, 1]
    v1 = _top16_rows(s1_ref, tr)
    v2 = _top16_rows(s2_ref, tr)
    sub = lax.broadcasted_iota(jnp.int32, (SUBLANES_, tr), 0)
    inf = jnp.float32(jnp.inf)

    def pack(vals):
        out = vals[SUBLANES_ - 1]
        for a in range(SUBLANES_ - 2, -1, -1):
            out = jnp.where(sub == a, vals[a], out)
        return out

    v1_lo = pack(v1[:SUBLANES_])
    v1_hi = pack(v1[SUBLANES_:])
    cand = [jnp.where(sub < k // (b + 1), v1_lo + v2[b], -inf) for b in range(k)]
    cand_hi = v1_hi + v2[0]
    merged = list(cand)
    merged[k - 1] = jnp.maximum(merged[k - 1], cand_hi)
    best = _merge_sublanes(_bitonic_merge_desc(merged))
    top = best[0]
    cut = best[k - 1]
    z = jnp.exp(best[0] - top)
    for b in range(1, k):
        z = z + jnp.exp(best[b] - top)
    rz = 1.0 / z

    t_lo = jnp.full((SUBLANES_, tr), inf, F32)
    for b in range(k):
        t_lo = jnp.minimum(t_lo, jnp.where(cand[b] >= cut, v2[b], inf))
    t_hi = jnp.where(cand_hi >= cut, v2[0], inf)
    t_rank = ([jnp.broadcast_to(t_lo[a:a + 1, :], (SUBLANES_, tr)) for a in range(SUBLANES_)]
              + [jnp.broadcast_to(t_hi[a:a + 1, :], (SUBLANES_, tr)) for a in range(SUBLANES_)])

    for v in range(PEER_NKEYS_ // SUBLANES_):
        rows = slice(v * SUBLANES_, (v + 1) * SUBLANES_)
        x1 = s1_ref[rows, :]
        theta = jnp.full((SUBLANES_, tr), inf, F32)
        for a in range(k):
            theta = jnp.where(x1 == v1[a], t_rank[a], theta)
        th_ref[0, rows, :] = theta
        c1_ref[0, rows, :] = jnp.exp(x1 - v1[0]) * rz
        e2_ref[0, rows, :] = jnp.exp(s2_ref[rows, :] - v2[0])


def _route(st4, tr):
    heads, _, nk, t = st4.shape
    out = jax.ShapeDtypeStruct((heads, nk, t), F32)
    spec = pl.BlockSpec((1, nk, tr), lambda h, j: (h, 0, j))
    return pl.pallas_call(
        _route_kernel,
        grid=(heads, t // tr),
        in_specs=[pl.BlockSpec((1, 2, nk, tr), lambda h, j: (h, 0, 0, j))],
        out_specs=(spec, spec, spec),
        out_shape=(out, out, out),
        compiler_params=_params(("parallel", "parallel")),
        name="peer_route",
    )(st4)


def _peer_kernel(ht_ref, u_ref, vt_ref, s2_ref, e2_ref, th_ref, c1_ref, x1_ref, gate_ref, o_ref,
                 acc_ref, g_scr, *, chunk):
    e = pl.program_id(1)
    te = u_ref.shape[0]

    @pl.when(e == 0)
    def _():
        acc_ref[...] = jnp.zeros_like(acc_ref)

    ht = ht_ref[...]
    for c in range(te // chunk):
        a = _dot(u_ref[c * chunk:(c + 1) * chunk, :], ht)
        for r in range(chunk // PEER_NKEYS_):
            i = c * (chunk // PEER_NKEYS_) + r
            ai = a[r * PEER_NKEYS_:(r + 1) * PEER_NKEYS_, :]
            w = None
            for h in range(PEER_HEADS_):
                hit = s2_ref[h, 0] >= th_ref[h, i:i + 1, :]
                term = jnp.where(hit, e2_ref[h], 0.0) * c1_ref[h, i:i + 1, :]
                w = term if w is None else w + term
            act = 0.5 * ai * (1.0 + lax.erf(ai * (2.0 ** -0.5)))
            g_scr[i * PEER_NKEYS_:(i + 1) * PEER_NKEYS_, :] = (w * act).astype(BF16)
    acc_ref[...] += _dot(vt_ref[...], g_scr[...])

    @pl.when(e == pl.num_programs(1) - 1)
    def _():
        o_ref[0] = x1_ref[0] + gate_ref[0] * acc_ref[...].T


def _peer(ht, u_bf, vt_bf, st4, e2, th, c1, x1, mod3, tt, te):
    d, t = ht.shape
    n_exp = u_bf.shape[0]
    b, l, _ = x1.shape
    per = l // tt
    rows = te // PEER_NKEYS_
    return pl.pallas_call(
        functools.partial(_peer_kernel, chunk=2 * PEER_NKEYS_),
        grid=(t // tt, n_exp // te),
        in_specs=[
            pl.BlockSpec((d, tt), lambda i, e: (0, i)),
            pl.BlockSpec((te, d), lambda i, e: (e, 0)),
            pl.BlockSpec((d, te), lambda i, e: (0, e)),
            pl.BlockSpec((PEER_HEADS_, 1, PEER_NKEYS_, tt), lambda i, e: (0, 1, 0, i)),
            pl.BlockSpec((PEER_HEADS_, PEER_NKEYS_, tt), lambda i, e: (0, 0, i)),
            pl.BlockSpec((PEER_HEADS_, rows, tt), lambda i, e: (0, e, i)),
            pl.BlockSpec((PEER_HEADS_, rows, tt), lambda i, e: (0, e, i)),
            pl.BlockSpec((1, tt, d), lambda i, e: (i // per, i % per, 0)),
            pl.BlockSpec((1, 1, d), lambda i, e: (i // per, 0, 5)),
        ],
        out_specs=pl.BlockSpec((1, tt, d), lambda i, e: (i // per, i % per, 0)),
        out_shape=jax.ShapeDtypeStruct((b, l, d), F32),
        scratch_shapes=[pltpu.VMEM((d, tt), F32), pltpu.VMEM((te, tt), BF16)],
        compiler_params=_params(("parallel", "arbitrary")),
        name="peer_dense",
    )(ht, u_bf, vt_bf, st4, e2, th, c1, x1, mod3)


def _rope_tables(seq):
    quarter = RET_DK_ // 4
    freqs = ROPE_BASE_ ** (-jnp.arange(quarter, dtype=F32) / quarter)
    rows = seq // GRID_W_
    row = jnp.repeat(jnp.arange(rows, dtype=F32), GRID_W_)
    col = jnp.tile(jnp.arange(GRID_W_, dtype=F32), rows)
    ar = row[:, None] * freqs
    ac = col[:, None] * freqs
    ang = jnp.concatenate([ar, ar, ac, ac], axis=-1)
    ang = jnp.concatenate([ang, ang], axis=-1)
    cos, sin = jnp.cos(ang), jnp.sin(ang)
    first = (jnp.arange(LANES_) % 32) < 16
    return cos, jnp.where(first, -sin, 0.0), jnp.where(first, 0.0, sin)


def _tiles(seq, ctx_len, tokens):
    pick = lambda n, pref: next(c for c in pref if n % c == 0)
    return dict(
        tm_in=pick(seq, (256, 128)),
        tm_ctx=pick(ctx_len, (256, 128)),
        tq=pick(seq, (512, 256, 128)),
        tk=pick(seq, (512, 256, 128)),
        tm_mix=pick(seq, (256, 128)),
        tr=pick(tokens, (256, 128)),
        tt=pick(seq, (512, 256, 128)),
        te=1024,
    )


def kernel(x, c, ctx, c_ctx, w_mod, b_mod, norm1_g, norm2_g, w_in, ret_decay_logit, ret_norm_g,
           diff_qk_norm_g, diff_lambda, diff_norm_g, w_out, peer_w_query, peer_sub_keys, peer_u, peer_v):
    assert w_mod.shape[0] == 1, "single-layer stack"
    b, l, d = x.shape
    ctx_len = ctx.shape[1]
    t = b * l
    tiles = _tiles(l, ctx_len, t)

    n_rows = -(-(b + 1) // SUBLANES_) * SUBLANES_
    cond = jnp.zeros((n_rows, d), F32).at[:b].set(c).at[b].set(c_ctx)
    mod = _adaln(cond, w_mod[0], b_mod[0])
    mod3 = mod.reshape(n_rows, 1, N_MOD_ * d)

    w_in_bf = w_in[0].astype(BF16)
    g1 = norm1_g[0].reshape(1, d)
    g2 = norm2_g[0].reshape(1, d)
    qkg = jnp.tile(diff_qk_norm_g[0], (1, DIFF_W_ // DIFF_D_))
    tabs = _rope_tables(l)
    rq, rk, rv, rg, dq, dk, dv = _inproj(x, mod3, lambda i: i, g1, w_in_bf, tabs, qkg, True,
                                         tiles["tm_in"])
    ctabs = tuple(jnp.zeros((ctx_len, LANES_), F32) for _ in range(3))
    _, crk, crv, _, _, cdk, cdv = _inproj(ctx, mod3, lambda i: b, g1, w_in_bf, ctabs, qkg, False,
                                          tiles["tm_ctx"])

    decay_b = jnp.broadcast_to(ret_decay_logit[0].T[:, :, None], (RET_HEADS_, 2, LANES_))
    ret = _retention(rq, rk, rv, rg, crk, crv, decay_b, ret_norm_g[0].reshape(RET_HEADS_, 1, RET_DV_))
    dif = _diffattn(dq, dk, dv, cdk, cdv, diff_lambda[0],
                    diff_norm_g[0].reshape(DIFF_HEADS_, 1, DIFF_DV_), tiles["tq"], tiles["tk"])

    wo = w_out[0].astype(BF16)
    wq_t = peer_w_query[0].T.astype(BF16)
    keys = peer_sub_keys[0].reshape(2 * PEER_HEADS_, PEER_NKEYS_, PEER_HALF_).astype(BF16)
    x1, ht, st = _mixout(ret, dif, x, mod3, g2, wo[:RET_W_], wo[RET_W_:], wq_t, keys, tiles["tm_mix"])

    st4 = st.reshape(PEER_HEADS_, 2, PEER_NKEYS_, t)
    th, c1, e2 = _route(st4, tiles["tr"])

    u_bf = peer_u[0].astype(BF16)
    vt_bf = peer_v[0].T.astype(BF16)
    return _peer(ht, u_bf, vt_bf, st4, e2, th, c1, x1, mod3, tiles["tt"], tiles["te"])
```

```python
import functools
import math

import jax
import jax.numpy as jnp
from jax import lax
from jax.experimental import pallas as pl
from jax.experimental.pallas import tpu as pltpu

F32 = jnp.float32
BF16 = jnp.bfloat16

D_MODEL_ = 1024
N_MOD_ = 6
NORM_EPS_ = 1e-6
RET_HEADS_ = 4
RET_DK_ = 64
RET_DV_ = 128
RET_CHUNK_ = 128
DIFF_HEADS_ = 4
DIFF_D_ = 64
DIFF_DV_ = 128
GRID_W_ = 64
ROPE_BASE_ = 10000.0
PEER_HEADS_ = 8
PEER_NKEYS_ = 128
PEER_HALF_ = 128
PEER_TOPK_ = 16
LAM_INIT_ = 0.8 - 0.6 * math.exp(-0.3 * 0)

LANES_ = 128
SUBLANES_ = 8
PACKED_LANES_ = 2 * LANES_
VMEM_LIMIT_BYTES_ = 56 * 1024 * 1024

RET_QK_ = RET_HEADS_ * RET_DK_
RET_W_ = RET_HEADS_ * RET_DV_
DIFF_W_ = DIFF_HEADS_ * DIFF_DV_
IN_COLS_ = 2 * RET_QK_ + 2 * RET_W_ + 3 * DIFF_W_


def _dot(a, b):
    return jnp.dot(a, b, preferred_element_type=F32)


def _dot_nt(a, b):
    return lax.dot_general(a, b, (((1,), (1,)), ((), ())), preferred_element_type=F32)


def _sigmoid(x):
    return 1.0 / (1.0 + jnp.exp(-x))


def _params(sem, flags=None):
    return pltpu.CompilerParams(dimension_semantics=sem, vmem_limit_bytes=VMEM_LIMIT_BYTES_, flags=flags)


def _adaln_kernel(cond_ref, w_ref, b_ref, o_ref):
    a = cond_ref[...]
    a = a * _sigmoid(a)
    o_ref[...] = _dot(a.astype(BF16), w_ref[...].astype(BF16)) + b_ref[...]


def _adaln(cond, w_mod, b_mod):
    rows, d = cond.shape
    n = w_mod.shape[1]
    tn = d
    return pl.pallas_call(
        _adaln_kernel,
        grid=(n // tn,),
        in_specs=[
            pl.BlockSpec((rows, d), lambda j: (0, 0)),
            pl.BlockSpec((d, tn), lambda j: (0, j)),
            pl.BlockSpec((1, tn), lambda j: (0, j)),
        ],
        out_specs=pl.BlockSpec((rows, tn), lambda j: (0, j)),
        out_shape=jax.ShapeDtypeStruct((rows, n), F32),
        compiler_params=_params(("parallel",)),
        name="adaln",
    )(cond, w_mod, b_mod.reshape(1, n))


def _rope_slab(v, cos, sin_up, sin_dn):
    up = pltpu.roll(v, LANES_ - 16, 1)
    dn = pltpu.roll(v, 16, 1)
    return v * cos + up * sin_up + dn * sin_dn


def _group_rms_scale(v, eps):
    w = v.shape[-1]
    sel = (lax.broadcasted_iota(jnp.int32, (w, LANES_), 0) // DIFF_D_
           == lax.broadcasted_iota(jnp.int32, (w, LANES_), 1)).astype(BF16)
    expand = (lax.broadcasted_iota(jnp.int32, (LANES_, w), 1) // DIFF_D_
              == lax.broadcasted_iota(jnp.int32, (LANES_, w), 0)).astype(BF16)
    v2 = v * v
    hi = v2.astype(BF16)
    lo = (v2 - hi.astype(F32)).astype(BF16)
    ssum = _dot(hi, sel) + _dot(lo, sel)
    r = lax.rsqrt(ssum * (1.0 / DIFF_D_) + eps)
    rhi = r.astype(BF16)
    rlo = (r - rhi.astype(F32)).astype(BF16)
    return _dot(rhi, expand) + _dot(rlo, expand)


def _inproj_kernel(x_ref, sh_ref, sc_ref, g_ref, w_ref, cos_ref, su_ref, sd_ref, qkg_ref,
                   rq_ref, rk_ref, rv_ref, rg_ref, dq_ref, dk_ref, dv_ref, y_scr, *, rope):
    x = x_ref[0]
    ms = jnp.mean(x * x, axis=-1, keepdims=True)
    h = x * lax.rsqrt(ms + NORM_EPS_) * g_ref[...]
    h = h * (1.0 + sc_ref[0]) + sh_ref[0]
    y_scr[...] = _dot(h.astype(BF16), w_ref[...])

    cos = cos_ref[...]
    su = su_ref[...]
    sd = sd_ref[...]

    def put(dst_ref, src_col, width, scale, norm_g):
        v = y_scr[:, src_col:src_col + width]
        if norm_g is not None:
            v = v * _group_rms_scale(v, NORM_EPS_) * norm_g
        for s in range(width // LANES_):
            slab = v[:, s * LANES_:(s + 1) * LANES_]
            if rope:
                slab = _rope_slab(slab, cos, su, sd)
            if scale != 1.0:
                slab = slab * scale
            dst_ref[0, :, s * LANES_:(s + 1) * LANES_] = slab.astype(dst_ref.dtype)

    c = 0
    put(rq_ref, c, RET_QK_, 1.0, None)
    c += RET_QK_
    put(rk_ref, c, RET_QK_, RET_DK_ ** -0.5, None)
    c += RET_QK_
    rv_ref[0] = y_scr[:, c:c + RET_W_].astype(BF16)
    c += RET_W_
    rg_ref[0] = y_scr[:, c:c + RET_W_]
    c += RET_W_
    put(dq_ref, c, DIFF_W_, 1.0, qkg_ref[0:1, :])
    c += DIFF_W_
    put(dk_ref, c, DIFF_W_, 1.0, qkg_ref[1:2, :])
    c += DIFF_W_
    dv_ref[0] = y_scr[:, c:c + DIFF_W_].astype(BF16)


def _inproj(xs, mod3, mod_row, norm_g, w_in_bf, tabs, qkg, rope, tm):
    b, l, d = xs.shape
    cos, su, sd = tabs
    grid = (b, l // tm)
    tok = lambda w: pl.BlockSpec((1, tm, w), lambda i, j: (i, j, 0))
    tab = pl.BlockSpec((tm, LANES_), lambda i, j: (j, 0))
    full = lambda a: pl.BlockSpec(a.shape, lambda i, j: (0,) * a.ndim)
    out_shapes = (
        jax.ShapeDtypeStruct((b, l, RET_QK_), BF16),
        jax.ShapeDtypeStruct((b, l, RET_QK_), BF16),
        jax.ShapeDtypeStruct((b, l, RET_W_), BF16),
        jax.ShapeDtypeStruct((b, l, RET_W_), F32),
        jax.ShapeDtypeStruct((b, l, DIFF_W_), BF16),
        jax.ShapeDtypeStruct((b, l, DIFF_W_), BF16),
        jax.ShapeDtypeStruct((b, l, DIFF_W_), BF16),
    )
    return pl.pallas_call(
        functools.partial(_inproj_kernel, rope=rope),
        grid=grid,
        in_specs=[
            tok(d),
            pl.BlockSpec((1, 1, d), lambda i, j: (mod_row(i), 0, 0)),
            pl.BlockSpec((1, 1, d), lambda i, j: (mod_row(i), 0, 1)),
            full(norm_g),
            full(w_in_bf),
            tab, tab, tab,
            full(qkg),
        ],
        out_specs=(tok(RET_QK_), tok(RET_QK_), tok(RET_W_), tok(RET_W_),
                   tok(DIFF_W_), tok(DIFF_W_), tok(DIFF_W_)),
        out_shape=out_shapes,
        scratch_shapes=[pltpu.VMEM((tm, IN_COLS_), F32)],
        compiler_params=_params(("parallel", "parallel")),
        name="inproj_rope" if rope else "inproj_ctx",
    )(xs, mod3, mod3, norm_g, w_in_bf, cos, su, sd, qkg)


def _log_sigmoid(x):
    return jnp.minimum(x, 0.0) - jnp.log(1.0 + jnp.exp(-jnp.abs(x)))


def _retention_kernel(dl_ref, q_ref, k_ref, v_ref, rg_ref, ck_ref, cv_ref, g_ref, o_ref,
                      lf_scr, lb_scr, rf_scr, rb_scr, *, seq, ctx_len):
    c = RET_CHUNK_
    n_chunks = seq // c
    head = pl.program_id(1)
    lane = lax.broadcasted_iota(jnp.int32, (1, LANES_), 1)
    head_lanes = (lane // RET_DK_) == (head % 2)

    dl = dl_ref[0]
    lg_f = _log_sigmoid(dl[0:1, :])
    lg_b = _log_sigmoid(dl[1:2, :])

    row = lax.broadcasted_iota(jnp.int32, (c, LANES_), 0).astype(F32)
    col = lax.broadcasted_iota(jnp.int32, (c, LANES_), 1).astype(F32)
    kw_f = jnp.exp((c - 1.0 - row) * lg_f)
    kw_b = jnp.exp(row * lg_b)
    qw_f = jnp.exp((row + 1.0) * lg_f)
    qw_b = jnp.exp((c - row) * lg_b)
    dist = row - col
    decay = jnp.where(dist >= 0.0,
                      jnp.exp(jnp.maximum(dist, 0.0) * lg_f),
                      jnp.exp(jnp.maximum(-dist, 0.0) * lg_b))
    chunk_f = jnp.exp(float(c) * lg_f)
    chunk_b = jnp.exp(float(c) * lg_b)

    crow = lax.broadcasted_iota(jnp.int32, (ctx_len, LANES_), 0).astype(F32)
    ck = ck_ref[0].astype(F32)
    cv = cv_ref[0]
    s_f = _dot((ck * jnp.exp((ctx_len - 1.0 - crow) * lg_f)).T.astype(BF16), cv)
    s_b = _dot((ck * jnp.exp(crow * lg_b)).T.astype(BF16), cv)

    def local_states(n, carry):
        sl = pl.ds(pl.multiple_of(n * c, c), c)
        kn = k_ref[0, sl, :].astype(F32)
        vn = v_ref[0, sl, :]
        lf_scr[n] = _dot((kn * kw_f).T.astype(BF16), vn)
        lb_scr[n] = _dot((kn * kw_b).T.astype(BF16), vn)
        return carry

    lax.fori_loop(0, n_chunks, local_states, 0)

    def scan_f(n, r):
        rf_scr[n] = r
        return chunk_f * r + lf_scr[n]

    lax.fori_loop(0, n_chunks, scan_f, s_f)

    def scan_b(m, r):
        n = n_chunks - 1 - m
        rb_scr[n] = r
        return chunk_b * r + lb_scr[n]

    lax.fori_loop(0, n_chunks, scan_b, s_b)

    gain = g_ref[0]

    def outputs(n, carry):
        sl = pl.ds(pl.multiple_of(n * c, c), c)
        qn = jnp.where(head_lanes, q_ref[0, sl, :], jnp.zeros((), BF16))
        scores = _dot_nt(qn, k_ref[0, sl, :])
        y = _dot((scores * decay).astype(BF16), v_ref[0, sl, :])
        y = y + qw_f * _dot(qn, rf_scr[n].astype(BF16)) + qw_b * _dot(qn, rb_scr[n].astype(BF16))
        ms = jnp.mean(y * y, axis=-1, keepdims=True)
        yn = y * lax.rsqrt(ms + NORM_EPS_) * gain
        rg = rg_ref[0, sl, :]
        o_ref[0, sl, :] = (yn * (rg * _sigmoid(rg))).astype(o_ref.dtype)
        return carry

    lax.fori_loop(0, n_chunks, outputs, 0)


def _retention(rq, rk, rv, rg, crk, crv, decay_b, norm_g3):
    b, l, _ = rq.shape
    ctx_len = crk.shape[1]
    n_chunks = l // RET_CHUNK_
    pair = lambda length: pl.BlockSpec((1, length, LANES_), lambda i, h: (i, 0, h // 2))
    own = lambda length: pl.BlockSpec((1, length, LANES_), lambda i, h: (i, 0, h))
    state = pltpu.VMEM((n_chunks, LANES_, RET_DV_), F32)
    return pl.pallas_call(
        functools.partial(_retention_kernel, seq=l, ctx_len=ctx_len),
        grid=(b, RET_HEADS_),
        in_specs=[
            pl.BlockSpec((1, 2, LANES_), lambda i, h: (h, 0, 0)),
            pair(l), pair(l), own(l), own(l), pair(ctx_len), own(ctx_len),
            pl.BlockSpec((1, 1, RET_DV_), lambda i, h: (h, 0, 0)),
        ],
        out_specs=own(l),
        out_shape=jax.ShapeDtypeStruct((b, l, RET_W_), BF16),
        scratch_shapes=[state, state, state, state],
        compiler_params=_params(("parallel", "parallel")),
        name="retention",
    )(decay_b, rq, rk, rv, rg, crk, crv, norm_g3)


def _diffattn_kernel(lam_ref, q_ref, k_ref, v_ref, ck_ref, cv_ref, g_ref, o_ref):
    q = q_ref[0]
    lane = lax.broadcasted_iota(jnp.int32, (1, LANES_), 1)
    first = lane < DIFF_D_
    zero = jnp.zeros((), BF16)
    q1 = jnp.where(first, q, zero)
    q2 = jnp.where(first, zero, q)
    scale = DIFF_D_ ** -0.5

    lp = lam_ref[...]
    lam = (jnp.exp(jnp.sum(lp[0:1] * lp[1:2], axis=-1, keepdims=True))
           - jnp.exp(jnp.sum(lp[2:3] * lp[3:4], axis=-1, keepdims=True)) + LAM_INIT_)

    def attend(qh):
        s_c = _dot_nt(qh, ck_ref[0]) * scale
        m_c = jnp.max(s_c, axis=-1, keepdims=True)
        p_c = jnp.exp(s_c - m_c)
        l_c = jnp.sum(p_c, axis=-1, keepdims=True)
        a_c = _dot(p_c.astype(BF16), cv_ref[0])
        s_l = _dot_nt(qh, k_ref[0]) * scale
        m = jnp.maximum(m_c, jnp.max(s_l, axis=-1, keepdims=True))
        alpha = jnp.exp(m_c - m)
        p_l = jnp.exp(s_l - m)
        denom = alpha * l_c + jnp.sum(p_l, axis=-1, keepdims=True)
        return (alpha * a_c + _dot(p_l.astype(BF16), v_ref[0])) / denom

    y = attend(q1) - lam * attend(q2)
    ms = jnp.mean(y * y, axis=-1, keepdims=True)
    yn = y * lax.rsqrt(ms + NORM_EPS_) * g_ref[0]
    o_ref[0] = (yn * (1.0 - LAM_INIT_)).astype(o_ref.dtype)


def _diffattn(dq, dk, dv, cdk, cdv, lam_params, norm_g3, tq):
    b, l, _ = dq.shape
    ctx_len = cdk.shape[1]
    kv = lambda length: pl.BlockSpec((1, length, LANES_), lambda i, h, j: (i, 0, h))
    qo = pl.BlockSpec((1, tq, LANES_), lambda i, h, j: (i, j, h))
    return pl.pallas_call(
        _diffattn_kernel,
        grid=(b, DIFF_HEADS_, l // tq),
        in_specs=[
            pl.BlockSpec(lam_params.shape, lambda i, h, j: (0, 0)),
            qo, kv(l), kv(l), kv(ctx_len), kv(ctx_len),
            pl.BlockSpec((1, 1, DIFF_DV_), lambda i, h, j: (h, 0, 0)),
        ],
        out_specs=qo,
        out_shape=jax.ShapeDtypeStruct((b, l, DIFF_W_), BF16),
        compiler_params=_params(("parallel", "parallel", "parallel")),
        name="diffattn",
    )(lam_params, dq, dk, dv, cdk, cdv, norm_g3)


def _mixout_kernel(ret_ref, dif_ref, x_ref, gate_ref, sh_ref, sc_ref, g_ref, woa_ref, wob_ref,
                   wqt_ref, keys_ref, x1_ref, ht_ref, st_ref):
    o = _dot(ret_ref[0], woa_ref[...]) + _dot(dif_ref[0], wob_ref[...])
    x1 = x_ref[0] + gate_ref[0] * o
    x1_ref[0] = x1
    ms = jnp.mean(x1 * x1, axis=-1, keepdims=True)
    h2 = x1 * lax.rsqrt(ms + NORM_EPS_) * g_ref[...]
    h2 = h2 * (1.0 + sc_ref[0]) + sh_ref[0]
    ht = h2.T.astype(BF16)
    ht_ref[...] = ht
    qt = _dot(wqt_ref[...], ht).astype(BF16)
    for hp in range(2 * PEER_HEADS_):
        st_ref[hp] = _dot(keys_ref[hp], qt[hp * PEER_HALF_:(hp + 1) * PEER_HALF_, :])


def _mixout(ret, dif, x, mod3, norm_g, wo_a, wo_b, wq_t, keys, tm):
    b, l, d = x.shape
    t = b * l
    per = l // tm
    tok = lambda w: pl.BlockSpec((1, tm, w), lambda i, j: (i, j, 0))
    modrow = lambda k: pl.BlockSpec((1, 1, d), lambda i, j: (i, 0, k))
    full = lambda a: pl.BlockSpec(a.shape, lambda i, j: (0,) * a.ndim)
    n_hp = 2 * PEER_HEADS_
    return pl.pallas_call(
        _mixout_kernel,
        grid=(b, per),
        in_specs=[tok(RET_W_), tok(DIFF_W_), tok(d), modrow(2), modrow(3), modrow(4),
                  full(norm_g), full(wo_a), full(wo_b), full(wq_t), full(keys)],
        out_specs=(
            tok(d),
            pl.BlockSpec((d, tm), lambda i, j: (0, i * per + j)),
            pl.BlockSpec((n_hp, PEER_NKEYS_, tm), lambda i, j: (0, 0, i * per + j)),
        ),
        out_shape=(
            jax.ShapeDtypeStruct((b, l, d), F32),
            jax.ShapeDtypeStruct((d, t), BF16),
            jax.ShapeDtypeStruct((n_hp, PEER_NKEYS_, t), F32),
        ),
        compiler_params=_params(("parallel", "parallel")),
        name="mix_out",
    )(ret, dif, x, mod3, mod3, mod3, norm_g, wo_a, wo_b, wq_t, keys)


def _sort16_pairs():
    def merge(lo, hi, r):
        step = r * 2
        if step < hi - lo:
            yield from merge(lo, hi, step)
            yield from merge(lo + r, hi, step)
            for i in range(lo + r, hi - r, step):
                yield (i, i + r)
        else:
            yield (lo, lo + r)

    def sort(lo, hi):
        if hi - lo >= 1:
            mid = lo + (hi - lo) // 2
            yield from sort(lo, mid)
            yield from sort(mid + 1, hi)
            yield from merge(lo, hi, 1)

    return tuple(sort(0, 15))


SORT16_PAIRS_ = _sort16_pairs()


def _bitonic_merge_desc(c):
    c = list(c)
    d = len(c) // 2
    while d >= 1:
        for k in range(len(c)):
            if k & d == 0:
                hi = jnp.maximum(c[k], c[k + d])
                lo = jnp.minimum(c[k], c[k + d])
                c[k], c[k + d] = hi, lo
        d //= 2
    return c


def _merge_top(a, b):
    n = len(a)
    return _bitonic_merge_desc([jnp.maximum(a[k], b[n - 1 - k]) for k in range(n)])


def _merge_sublanes(a):
    for shift in (4, 2, 1):
        a = _merge_top(a, [pltpu.roll(v, shift, 0) for v in a])
    return a


def _top16_rows(x):
    n = PEER_NKEYS_ // SUBLANES_
    a = [x[v * SUBLANES_:(v + 1) * SUBLANES_, :] for v in range(n)]
    for i, j in SORT16_PAIRS_:
        hi = jnp.maximum(a[i], a[j])
        lo = jnp.minimum(a[i], a[j])
        a[i], a[j] = hi, lo
    return _merge_sublanes(a)


def _route_group(s1, s2):
    w = s1.shape[-1]
    k = PEER_TOPK_
    v1 = _top16_rows(s1)
    v2 = _top16_rows(s2)
    sub = lax.broadcasted_iota(jnp.int32, (SUBLANES_, w), 0)
    inf = jnp.float32(jnp.inf)

    def pack(vals):
        out = vals[SUBLANES_ - 1]
        for a in range(SUBLANES_ - 2, -1, -1):
            out = jnp.where(sub == a, vals[a], out)
        return out

    v1_lo = pack(v1[:SUBLANES_])
    v1_hi = pack(v1[SUBLANES_:])
    cand = [jnp.where(sub < k // (b + 1), v1_lo + v2[b], -inf) for b in range(k)]
    cand_hi = v1_hi + v2[0]
    merged = list(cand)
    merged[k - 1] = jnp.maximum(merged[k - 1], cand_hi)
    best = _merge_sublanes(_bitonic_merge_desc(merged))
    top = best[0]
    cut = best[k - 1]
    z = jnp.exp(best[0] - top)
    for b in range(1, k):
        z = z + jnp.exp(best[b] - top)
    rz = 1.0 / z

    n_lo = jnp.zeros((SUBLANES_, w), F32)
    for b in range(k):
        n_lo = n_lo + jnp.where(cand[b] >= cut, 1.0, 0.0)
    n_hi = jnp.where(cand_hi >= cut, 1.0, 0.0)
    n_rank = ([jnp.broadcast_to(n_lo[a:a + 1, :], (SUBLANES_, w)) for a in range(SUBLANES_)]
              + [jnp.broadcast_to(n_hi[a:a + 1, :], (SUBLANES_, w)) for a in range(SUBLANES_)])

    rank2, e2, count, c1 = [], [], [], []
    for v in range(PEER_NKEYS_ // SUBLANES_):
        rows = slice(v * SUBLANES_, (v + 1) * SUBLANES_)
        x1 = s1[rows, :]
        x2 = s2[rows, :]
        cnt = jnp.zeros((SUBLANES_, w), F32)
        for a in range(k):
            cnt = jnp.where(x1 == v1[a], n_rank[a], cnt)
        rk = jnp.full((SUBLANES_, w), float(k), F32)
        for b in range(k - 1, -1, -1):
            rk = jnp.where(x2 >= v2[b], float(b), rk)
        count.append(cnt)
        c1.append(jnp.exp(x1 - v1[0]) * rz)
        rank2.append(rk)
        e2.append(jnp.exp(x2 - v2[0]))
    cat = lambda parts: jnp.concatenate(parts, axis=0)
    return cat(rank2), cat(e2), cat(count), cat(c1)


def _route_kernel(s_ref, r2_ref, e2_ref, n_ref, c1_ref, *, group):
    tr = s_ref.shape[-1]
    packed = (PEER_NKEYS_ // (2 * SUBLANES_), 2 * SUBLANES_, group)

    def body(g, carry):
        cols = pl.ds(pl.multiple_of(g * group, group), group)
        rank2, e2, count, c1 = _route_group(s_ref[0, 0, :, cols], s_ref[0, 1, :, cols])
        r2_ref[0, :, :, cols] = rank2.astype(BF16).reshape(packed)
        e2_ref[0, :, :, cols] = e2.astype(BF16).reshape(packed)
        n_ref[0, :, cols] = count
        c1_ref[0, :, cols] = c1
        return carry

    lax.fori_loop(0, tr // group, body, 0)


def _route(st4, tr, group):
    heads, _, nk, t = st4.shape
    pk = (nk // (2 * SUBLANES_), 2 * SUBLANES_)
    out_pk = jax.ShapeDtypeStruct((heads,) + pk + (t,), BF16)
    out_f = jax.ShapeDtypeStruct((heads, nk, t), F32)
    spec_pk = pl.BlockSpec((1,) + pk + (tr,), lambda h, j: (h, 0, 0, j))
    spec_f = pl.BlockSpec((1, nk, tr), lambda h, j: (h, 0, j))
    return pl.pallas_call(
        functools.partial(_route_kernel, group=group),
        grid=(heads, t // tr),
        in_specs=[pl.BlockSpec((1, 2, nk, tr), lambda h, j: (h, 0, 0, j))],
        out_specs=(spec_pk, spec_pk, spec_f, spec_f),
        out_shape=(out_pk, out_pk, out_f, out_f),
        compiler_params=_params(("parallel", "parallel")),
        name="peer_route",
    )(st4)


def _peer_kernel(ht_ref, u_ref, vtp_ref, vtc_ref, r2_ref, e2_ref, n_ref, c1_ref, x1_ref, gate_ref, o_ref,
                 acc_ref, g_scr, nb_scr, cb_scr, *, chunk, n_blocks):
    e = pl.program_id(1)
    te = u_ref.shape[0]
    tt = ht_ref.shape[1]
    pk = 2 * SUBLANES_
    rows_per_chunk = chunk // PEER_NKEYS_
    zero = jnp.zeros((), BF16)
    n_chunks = te // chunk
    groups = PEER_NKEYS_ // pk

    def fold(vt_ref, c):
        acc_ref[...] += _dot(vt_ref[:, c * chunk:(c + 1) * chunk], g_scr[c].reshape(chunk, tt))

    def weighted_activations(c, a):
        act = (0.5 * a * (1.0 + lax.erf(a * (2.0 ** -0.5)))).astype(BF16)
        act = act.reshape(rows_per_chunk * groups, pk, tt)
        for r in range(rows_per_chunk):
            i = c * rows_per_chunk + r
            for h in range(PEER_HEADS_):
                nb_scr[h, i] = jnp.broadcast_to(n_ref[h, i:i + 1, :], (pk, tt)).astype(BF16)
                cb_scr[h, i] = jnp.broadcast_to(c1_ref[h, i:i + 1, :], (pk, tt)).astype(BF16)
            for lg in range(tt // PACKED_LANES_):
                cols = slice(lg * PACKED_LANES_, (lg + 1) * PACKED_LANES_)
                cnts = [nb_scr[h, i:i + 1, :, cols] for h in range(PEER_HEADS_)]
                c1s = [cb_scr[h, i:i + 1, :, cols] for h in range(PEER_HEADS_)]
                for s in range(groups):
                    w = None
                    for h in range(PEER_HEADS_):
                        term = jnp.where(r2_ref[h, s:s + 1, :, cols] < cnts[h],
                                         e2_ref[h, s:s + 1, :, cols], zero) * c1s[h]
                        w = term if w is None else w + term
                    g_scr[c, r * groups + s:r * groups + s + 1, :, cols] = (
                        w * act[r * groups + s:r * groups + s + 1, :, cols])

    def step(first):
        ht = ht_ref[...]
        first_dot = lambda c: _dot(u_ref[c * chunk:(c + 1) * chunk, :], ht)
        a_next = first_dot(0)
        for c in range(n_chunks):
            a = a_next
            if c + 1 < n_chunks:
                a_next = first_dot(c + 1)
            if c > 0:
                fold(vtc_ref, c - 1)
            elif not first:
                fold(vtp_ref, n_chunks - 1)
            weighted_activations(c, a)

    @pl.when(e == 0)
    def _():
        acc_ref[...] = jnp.zeros_like(acc_ref)
        step(True)

    @pl.when((e > 0) & (e < n_blocks))
    def _():
        step(False)

    @pl.when(e == n_blocks)
    def _():
        fold(vtp_ref, n_chunks - 1)
        o_ref[0] = x1_ref[0] + gate_ref[0] * acc_ref[...].T


def _peer(ht, u_bf, vt_bf, rank2, e2, count, c1, x1, mod3, tt, te):
    d, t = ht.shape
    n_exp = u_bf.shape[0]
    b, l, _ = x1.shape
    per = l // tt
    rows = te // PEER_NKEYS_
    pk = (PEER_NKEYS_ // (2 * SUBLANES_), 2 * SUBLANES_)
    chunk = 2 * PEER_NKEYS_
    n_blocks = n_exp // te
    g_scratch = pltpu.VMEM((te // chunk, chunk // pk[1], pk[1], tt), BF16)
    cur = lambda e: jnp.minimum(e, n_blocks - 1)
    prev = lambda e: jnp.maximum(e - 1, 0)
    return pl.pallas_call(
        functools.partial(_peer_kernel, chunk=chunk, n_blocks=n_blocks),
        grid=(t // tt, n_blocks + 1),
        in_specs=[
            pl.BlockSpec((d, tt), lambda i, e: (0, i)),
            pl.BlockSpec((te, d), lambda i, e: (cur(e), 0)),
            pl.BlockSpec((d, te), lambda i, e: (0, prev(e))),
            pl.BlockSpec((d, te), lambda i, e: (0, cur(e))),
            pl.BlockSpec((PEER_HEADS_,) + pk + (tt,), lambda i, e: (0, 0, 0, i)),
            pl.BlockSpec((PEER_HEADS_,) + pk + (tt,), lambda i, e: (0, 0, 0, i)),
            pl.BlockSpec((PEER_HEADS_, rows, tt), lambda i, e: (0, cur(e), i)),
            pl.BlockSpec((PEER_HEADS_, rows, tt), lambda i, e: (0, cur(e), i)),
            pl.BlockSpec((1, tt, d), lambda i, e: (i // per, i % per, 0)),
            pl.BlockSpec((1, 1, d), lambda i, e: (i // per, 0, 5)),
        ],
        out_specs=pl.BlockSpec((1, tt, d), lambda i, e: (i // per, i % per, 0)),
        out_shape=jax.ShapeDtypeStruct((b, l, d), F32),
        scratch_shapes=[pltpu.VMEM((d, tt), F32), g_scratch,
                        pltpu.VMEM((PEER_HEADS_, rows, pk[1], tt), BF16),
                        pltpu.VMEM((PEER_HEADS_, rows, pk[1], tt), BF16)],
        compiler_params=_params(("parallel", "arbitrary")),
        name="peer_dense",
    )(ht, u_bf, vt_bf, vt_bf, rank2, e2, count, c1, x1, mod3)


def _rope_tables(seq):
    quarter = RET_DK_ // 4
    freqs = ROPE_BASE_ ** (-jnp.arange(quarter, dtype=F32) / quarter)
    rows = seq // GRID_W_
    row = jnp.repeat(jnp.arange(rows, dtype=F32), GRID_W_)
    col = jnp.tile(jnp.arange(GRID_W_, dtype=F32), rows)
    ar = row[:, None] * freqs
    ac = col[:, None] * freqs
    ang = jnp.concatenate([ar, ar, ac, ac], axis=-1)
    ang = jnp.concatenate([ang, ang], axis=-1)
    cos, sin = jnp.cos(ang), jnp.sin(ang)
    first = (jnp.arange(LANES_) % 32) < 16
    return cos, jnp.where(first, -sin, 0.0), jnp.where(first, 0.0, sin)


def _tiles(seq, ctx_len, tokens):
    pick = lambda n, pref: next(c for c in pref if n % c == 0)
    return dict(
        tm_in=pick(seq, (256, 128)),
        tm_ctx=pick(ctx_len, (256, 128)),
        tq=pick(seq, (512, 256, 128)),
        tm_mix=pick(seq, (256, 128)),
        tr=pick(tokens, (1024, 512, 256, 128)),
        route_group=pick(tokens, (256, 128)),
        tt=pick(seq, (512, 256, 128)),
        te=1024,
    )


def kernel(x, c, ctx, c_ctx, w_mod, b_mod, norm1_g, norm2_g, w_in, ret_decay_logit, ret_norm_g,
           diff_qk_norm_g, diff_lambda, diff_norm_g, w_out, peer_w_query, peer_sub_keys, peer_u, peer_v):
    assert w_mod.shape[0] == 1, "single-layer stack"
    b, l, d = x.shape
    ctx_len = ctx.shape[1]
    t = b * l
    tiles = _tiles(l, ctx_len, t)

    n_rows = -(-(b + 1) // SUBLANES_) * SUBLANES_
    cond = jnp.zeros((n_rows, d), F32).at[:b].set(c).at[b].set(c_ctx)
    mod = _adaln(cond, w_mod[0], b_mod[0])
    mod3 = mod.reshape(n_rows, 1, N_MOD_ * d)

    w_in_bf = w_in[0].astype(BF16)
    g1 = norm1_g[0].reshape(1, d)
    g2 = norm2_g[0].reshape(1, d)
    qkg = jnp.tile(diff_qk_norm_g[0], (1, DIFF_W_ // DIFF_D_))
    tabs = _rope_tables(l)
    rq, rk, rv, rg, dq, dk, dv = _inproj(x, mod3, lambda i: i, g1, w_in_bf, tabs, qkg, True,
                                         tiles["tm_in"])
    ctabs = tuple(jnp.zeros((ctx_len, LANES_), F32) for _ in range(3))
    _, crk, crv, _, _, cdk, cdv = _inproj(ctx, mod3, lambda i: b, g1, w_in_bf, ctabs, qkg, False,
                                          tiles["tm_ctx"])

    decay_b = jnp.broadcast_to(ret_decay_logit[0].T[:, :, None], (RET_HEADS_, 2, LANES_))
    ret = _retention(rq, rk, rv, rg, crk, crv, decay_b, ret_norm_g[0].reshape(RET_HEADS_, 1, RET_DV_))
    dif = _diffattn(dq, dk, dv, cdk, cdv, diff_lambda[0],
                    diff_norm_g[0].reshape(DIFF_HEADS_, 1, DIFF_DV_), tiles["tq"])

    wo = w_out[0].astype(BF16)
    wq_t = peer_w_query[0].T.astype(BF16)
    keys = peer_sub_keys[0].reshape(2 * PEER_HEADS_, PEER_NKEYS_, PEER_HALF_).astype(BF16)
    x1, ht, st = _mixout(ret, dif, x, mod3, g2, wo[:RET_W_], wo[RET_W_:], wq_t, keys, tiles["tm_mix"])

    st4 = st.reshape(PEER_HEADS_, 2, PEER_NKEYS_, t)
    rank2, e2, count, c1 = _route(st4, tiles["tr"], tiles["route_group"])

    u_bf = peer_u[0].astype(BF16)
    vt_bf = peer_v[0].T.astype(BF16)
    return _peer(ht, u_bf, vt_bf, rank2, e2, count, c1, x1, mod3, tiles["tt"], tiles["te"])
```

```python
import functools
import math

import jax
import jax.numpy as jnp
from jax import lax
from jax.experimental import pallas as pl
from jax.experimental.pallas import tpu as pltpu

F32 = jnp.float32
BF16 = jnp.bfloat16

D_MODEL_ = 1024
N_MOD_ = 6
NORM_EPS_ = 1e-6
RET_HEADS_ = 4
RET_DK_ = 64
RET_DV_ = 128
RET_CHUNK_ = 128
DIFF_HEADS_ = 4
DIFF_D_ = 64
DIFF_DV_ = 128
GRID_W_ = 64
ROPE_BASE_ = 10000.0
PEER_HEADS_ = 8
PEER_NKEYS_ = 128
PEER_HALF_ = 128
PEER_TOPK_ = 16
LAM_INIT_ = 0.8 - 0.6 * math.exp(-0.3 * 0)

LANES_ = 128
SUBLANES_ = 8
PACKED_LANES_ = 2 * LANES_
VMEM_LIMIT_BYTES_ = 56 * 1024 * 1024

RET_QK_ = RET_HEADS_ * RET_DK_
RET_W_ = RET_HEADS_ * RET_DV_
DIFF_W_ = DIFF_HEADS_ * DIFF_DV_
IN_COLS_ = 2 * RET_QK_ + 2 * RET_W_ + 3 * DIFF_W_


def _dot(a, b):
    return jnp.dot(a, b, preferred_element_type=F32)


def _dot_nt(a, b):
    return lax.dot_general(a, b, (((1,), (1,)), ((), ())), preferred_element_type=F32)


def _sigmoid(x):
    return 1.0 / (1.0 + jnp.exp(-x))


def _params(sem, flags=None):
    return pltpu.CompilerParams(dimension_semantics=sem, vmem_limit_bytes=VMEM_LIMIT_BYTES_, flags=flags)


def _adaln_kernel(cond_ref, w_ref, b_ref, o_ref):
    a = cond_ref[...]
    a = a * _sigmoid(a)
    o_ref[...] = _dot(a.astype(BF16), w_ref[...].astype(BF16)) + b_ref[...]


def _adaln(cond, w_mod, b_mod):
    rows, d = cond.shape
    n = w_mod.shape[1]
    tn = d
    return pl.pallas_call(
        _adaln_kernel,
        grid=(n // tn,),
        in_specs=[
            pl.BlockSpec((rows, d), lambda j: (0, 0)),
            pl.BlockSpec((d, tn), lambda j: (0, j)),
            pl.BlockSpec((1, tn), lambda j: (0, j)),
        ],
        out_specs=pl.BlockSpec((rows, tn), lambda j: (0, j)),
        out_shape=jax.ShapeDtypeStruct((rows, n), F32),
        compiler_params=_params(("parallel",)),
        name="adaln",
    )(cond, w_mod, b_mod.reshape(1, n))


def _rope_slab(v, cos, sin_up, sin_dn):
    up = pltpu.roll(v, LANES_ - 16, 1)
    dn = pltpu.roll(v, 16, 1)
    return v * cos + up * sin_up + dn * sin_dn


def _group_rms_scale(v, eps):
    w = v.shape[-1]
    sel = (lax.broadcasted_iota(jnp.int32, (w, LANES_), 0) // DIFF_D_
           == lax.broadcasted_iota(jnp.int32, (w, LANES_), 1)).astype(BF16)
    expand = (lax.broadcasted_iota(jnp.int32, (LANES_, w), 1) // DIFF_D_
              == lax.broadcasted_iota(jnp.int32, (LANES_, w), 0)).astype(BF16)
    v2 = v * v
    hi = v2.astype(BF16)
    lo = (v2 - hi.astype(F32)).astype(BF16)
    ssum = _dot(hi, sel) + _dot(lo, sel)
    r = lax.rsqrt(ssum * (1.0 / DIFF_D_) + eps)
    rhi = r.astype(BF16)
    rlo = (r - rhi.astype(F32)).astype(BF16)
    return _dot(rhi, expand) + _dot(rlo, expand)


def _inproj_kernel(x_ref, sh_ref, sc_ref, g_ref, w_ref, cos_ref, su_ref, sd_ref, qkg_ref,
                   rq_ref, rk_ref, rv_ref, rg_ref, dq_ref, dk_ref, dv_ref, y_scr, *, rope):
    x = x_ref[0]
    ms = jnp.mean(x * x, axis=-1, keepdims=True)
    h = x * lax.rsqrt(ms + NORM_EPS_) * g_ref[...]
    h = h * (1.0 + sc_ref[0]) + sh_ref[0]
    y_scr[...] = _dot(h.astype(BF16), w_ref[...])

    cos = cos_ref[...]
    su = su_ref[...]
    sd = sd_ref[...]

    def put(dst_ref, src_col, width, scale, norm_g):
        v = y_scr[:, src_col:src_col + width]
        if norm_g is not None:
            v = v * _group_rms_scale(v, NORM_EPS_) * norm_g
        for s in range(width // LANES_):
            slab = v[:, s * LANES_:(s + 1) * LANES_]
            if rope:
                slab = _rope_slab(slab, cos, su, sd)
            if scale != 1.0:
                slab = slab * scale
            dst_ref[0, :, s * LANES_:(s + 1) * LANES_] = slab.astype(dst_ref.dtype)

    c = 0
    put(rq_ref, c, RET_QK_, 1.0, None)
    c += RET_QK_
    put(rk_ref, c, RET_QK_, RET_DK_ ** -0.5, None)
    c += RET_QK_
    rv_ref[0] = y_scr[:, c:c + RET_W_].astype(BF16)
    c += RET_W_
    rg_ref[0] = y_scr[:, c:c + RET_W_]
    c += RET_W_
    put(dq_ref, c, DIFF_W_, 1.0, qkg_ref[0:1, :])
    c += DIFF_W_
    put(dk_ref, c, DIFF_W_, 1.0, qkg_ref[1:2, :])
    c += DIFF_W_
    dv_ref[0] = y_scr[:, c:c + DIFF_W_].astype(BF16)


def _inproj(xs, mod3, mod_row, norm_g, w_in_bf, tabs, qkg, rope, tm):
    b, l, d = xs.shape
    cos, su, sd = tabs
    grid = (b, l // tm)
    tok = lambda w: pl.BlockSpec((1, tm, w), lambda i, j: (i, j, 0))
    tab = pl.BlockSpec((tm, LANES_), lambda i, j: (j, 0))
    full = lambda a: pl.BlockSpec(a.shape, lambda i, j: (0,) * a.ndim)
    out_shapes = (
        jax.ShapeDtypeStruct((b, l, RET_QK_), BF16),
        jax.ShapeDtypeStruct((b, l, RET_QK_), BF16),
        jax.ShapeDtypeStruct((b, l, RET_W_), BF16),
        jax.ShapeDtypeStruct((b, l, RET_W_), F32),
        jax.ShapeDtypeStruct((b, l, DIFF_W_), BF16),
        jax.ShapeDtypeStruct((b, l, DIFF_W_), BF16),
        jax.ShapeDtypeStruct((b, l, DIFF_W_), BF16),
    )
    return pl.pallas_call(
        functools.partial(_inproj_kernel, rope=rope),
        grid=grid,
        in_specs=[
            tok(d),
            pl.BlockSpec((1, 1, d), lambda i, j: (mod_row(i), 0, 0)),
            pl.BlockSpec((1, 1, d), lambda i, j: (mod_row(i), 0, 1)),
            full(norm_g),
            full(w_in_bf),
            tab, tab, tab,
            full(qkg),
        ],
        out_specs=(tok(RET_QK_), tok(RET_QK_), tok(RET_W_), tok(RET_W_),
                   tok(DIFF_W_), tok(DIFF_W_), tok(DIFF_W_)),
        out_shape=out_shapes,
        scratch_shapes=[pltpu.VMEM((tm, IN_COLS_), F32)],
        compiler_params=_params(("parallel", "parallel")),
        name="inproj_rope" if rope else "inproj_ctx",
    )(xs, mod3, mod3, norm_g, w_in_bf, cos, su, sd, qkg)


def _log_sigmoid(x):
    return jnp.minimum(x, 0.0) - jnp.log(1.0 + jnp.exp(-jnp.abs(x)))


def _retention_kernel(dl_ref, q_ref, k_ref, v_ref, rg_ref, ck_ref, cv_ref, g_ref, o_ref,
                      lf_scr, lb_scr, rf_scr, rb_scr, *, seq, ctx_len):
    c = RET_CHUNK_
    n_chunks = seq // c
    head = pl.program_id(1)
    lane = lax.broadcasted_iota(jnp.int32, (1, LANES_), 1)
    head_lanes = (lane // RET_DK_) == (head % 2)

    dl = dl_ref[0]
    lg_f = _log_sigmoid(dl[0:1, :])
    lg_b = _log_sigmoid(dl[1:2, :])

    row = lax.broadcasted_iota(jnp.int32, (c, LANES_), 0).astype(F32)
    col = lax.broadcasted_iota(jnp.int32, (c, LANES_), 1).astype(F32)
    kw_f = jnp.exp((c - 1.0 - row) * lg_f)
    kw_b = jnp.exp(row * lg_b)
    qw_f = jnp.exp((row + 1.0) * lg_f)
    qw_b = jnp.exp((c - row) * lg_b)
    dist = row - col
    decay = jnp.where(dist >= 0.0,
                      jnp.exp(jnp.maximum(dist, 0.0) * lg_f),
                      jnp.exp(jnp.maximum(-dist, 0.0) * lg_b))
    chunk_f = jnp.exp(float(c) * lg_f)
    chunk_b = jnp.exp(float(c) * lg_b)

    crow = lax.broadcasted_iota(jnp.int32, (ctx_len, LANES_), 0).astype(F32)
    ck = ck_ref[0].astype(F32)
    cv = cv_ref[0]
    s_f = _dot((ck * jnp.exp((ctx_len - 1.0 - crow) * lg_f)).T.astype(BF16), cv)
    s_b = _dot((ck * jnp.exp(crow * lg_b)).T.astype(BF16), cv)

    def local_states(n, carry):
        sl = pl.ds(pl.multiple_of(n * c, c), c)
        kn = k_ref[0, sl, :].astype(F32)
        vn = v_ref[0, sl, :]
        lf_scr[n] = _dot((kn * kw_f).T.astype(BF16), vn)
        lb_scr[n] = _dot((kn * kw_b).T.astype(BF16), vn)
        return carry

    lax.fori_loop(0, n_chunks, local_states, 0, unroll=True)

    def scan_f(n, r):
        rf_scr[n] = r
        return chunk_f * r + lf_scr[n]

    lax.fori_loop(0, n_chunks, scan_f, s_f)

    def scan_b(m, r):
        n = n_chunks - 1 - m
        rb_scr[n] = r
        return chunk_b * r + lb_scr[n]

    lax.fori_loop(0, n_chunks, scan_b, s_b)

    gain = g_ref[0]

    def outputs(n, carry):
        sl = pl.ds(pl.multiple_of(n * c, c), c)
        qn = jnp.where(head_lanes, q_ref[0, sl, :], jnp.zeros((), BF16))
        scores = _dot_nt(qn, k_ref[0, sl, :])
        y = _dot((scores * decay).astype(BF16), v_ref[0, sl, :])
        y = y + qw_f * _dot(qn, rf_scr[n].astype(BF16)) + qw_b * _dot(qn, rb_scr[n].astype(BF16))
        ms = jnp.mean(y * y, axis=-1, keepdims=True)
        yn = y * lax.rsqrt(ms + NORM_EPS_) * gain
        rg = rg_ref[0, sl, :]
        o_ref[0, sl, :] = (yn * (rg * _sigmoid(rg))).astype(o_ref.dtype)
        return carry

    lax.fori_loop(0, n_chunks, outputs, 0, unroll=True)


def _retention(rq, rk, rv, rg, crk, crv, decay_b, norm_g3):
    b, l, _ = rq.shape
    ctx_len = crk.shape[1]
    n_chunks = l // RET_CHUNK_
    pair = lambda length: pl.BlockSpec((1, length, LANES_), lambda i, h: (i, 0, h // 2))
    own = lambda length: pl.BlockSpec((1, length, LANES_), lambda i, h: (i, 0, h))
    state = pltpu.VMEM((n_chunks, LANES_, RET_DV_), F32)
    return pl.pallas_call(
        functools.partial(_retention_kernel, seq=l, ctx_len=ctx_len),
        grid=(b, RET_HEADS_),
        in_specs=[
            pl.BlockSpec((1, 2, LANES_), lambda i, h: (h, 0, 0)),
            pair(l), pair(l), own(l), own(l), pair(ctx_len), own(ctx_len),
            pl.BlockSpec((1, 1, RET_DV_), lambda i, h: (h, 0, 0)),
        ],
        out_specs=own(l),
        out_shape=jax.ShapeDtypeStruct((b, l, RET_W_), BF16),
        scratch_shapes=[state, state, state, state],
        compiler_params=_params(("parallel", "parallel")),
        name="retention",
    )(decay_b, rq, rk, rv, rg, crk, crv, norm_g3)


def _diffattn_kernel(lam_ref, q_ref, k_ref, v_ref, ck_ref, cv_ref, g_ref, o_ref):
    q = q_ref[0]
    lane = lax.broadcasted_iota(jnp.int32, (1, LANES_), 1)
    first = lane < DIFF_D_
    zero = jnp.zeros((), BF16)
    q1 = jnp.where(first, q, zero)
    q2 = jnp.where(first, zero, q)
    scale = DIFF_D_ ** -0.5

    lp = lam_ref[...]
    lam = (jnp.exp(jnp.sum(lp[0:1] * lp[1:2], axis=-1, keepdims=True))
           - jnp.exp(jnp.sum(lp[2:3] * lp[3:4], axis=-1, keepdims=True)) + LAM_INIT_)

    def attend(qh):
        s_c = _dot_nt(qh, ck_ref[0]) * scale
        m_c = jnp.max(s_c, axis=-1, keepdims=True)
        p_c = jnp.exp(s_c - m_c)
        l_c = jnp.sum(p_c, axis=-1, keepdims=True)
        a_c = _dot(p_c.astype(BF16), cv_ref[0])
        s_l = _dot_nt(qh, k_ref[0]) * scale
        m = jnp.maximum(m_c, jnp.max(s_l, axis=-1, keepdims=True))
        alpha = jnp.exp(m_c - m)
        p_l = jnp.exp(s_l - m)
        denom = alpha * l_c + jnp.sum(p_l, axis=-1, keepdims=True)
        return (alpha * a_c + _dot(p_l.astype(BF16), v_ref[0])) / denom

    y = attend(q1) - lam * attend(q2)
    ms = jnp.mean(y * y, axis=-1, keepdims=True)
    yn = y * lax.rsqrt(ms + NORM_EPS_) * g_ref[0]
    o_ref[0] = (yn * (1.0 - LAM_INIT_)).astype(o_ref.dtype)


def _diffattn(dq, dk, dv, cdk, cdv, lam_params, norm_g3, tq):
    b, l, _ = dq.shape
    ctx_len = cdk.shape[1]
    kv = lambda length: pl.BlockSpec((1, length, LANES_), lambda i, h, j: (i, 0, h))
    qo = pl.BlockSpec((1, tq, LANES_), lambda i, h, j: (i, j, h))
    return pl.pallas_call(
        _diffattn_kernel,
        grid=(b, DIFF_HEADS_, l // tq),
        in_specs=[
            pl.BlockSpec(lam_params.shape, lambda i, h, j: (0, 0)),
            qo, kv(l), kv(l), kv(ctx_len), kv(ctx_len),
            pl.BlockSpec((1, 1, DIFF_DV_), lambda i, h, j: (h, 0, 0)),
        ],
        out_specs=qo,
        out_shape=jax.ShapeDtypeStruct((b, l, DIFF_W_), BF16),
        compiler_params=_params(("parallel", "parallel", "parallel")),
        name="diffattn",
    )(lam_params, dq, dk, dv, cdk, cdv, norm_g3)


def _mixout_kernel(ret_ref, dif_ref, x_ref, gate_ref, sh_ref, sc_ref, g_ref, woa_ref, wob_ref,
                   wqt_ref, keys_ref, x1_ref, ht_ref, st_ref):
    o = _dot(ret_ref[0], woa_ref[...]) + _dot(dif_ref[0], wob_ref[...])
    x1 = x_ref[0] + gate_ref[0] * o
    x1_ref[0] = x1
    ms = jnp.mean(x1 * x1, axis=-1, keepdims=True)
    h2 = x1 * lax.rsqrt(ms + NORM_EPS_) * g_ref[...]
    h2 = h2 * (1.0 + sc_ref[0]) + sh_ref[0]
    ht = h2.T.astype(BF16)
    ht_ref[...] = ht
    qt = _dot(wqt_ref[...], ht).astype(BF16)
    for hp in range(2 * PEER_HEADS_):
        st_ref[hp] = _dot(keys_ref[hp], qt[hp * PEER_HALF_:(hp + 1) * PEER_HALF_, :])


def _mixout(ret, dif, x, mod3, norm_g, wo_a, wo_b, wq_t, keys, tm):
    b, l, d = x.shape
    t = b * l
    per = l // tm
    tok = lambda w: pl.BlockSpec((1, tm, w), lambda i, j: (i, j, 0))
    modrow = lambda k: pl.BlockSpec((1, 1, d), lambda i, j: (i, 0, k))
    full = lambda a: pl.BlockSpec(a.shape, lambda i, j: (0,) * a.ndim)
    n_hp = 2 * PEER_HEADS_
    return pl.pallas_call(
        _mixout_kernel,
        grid=(b, per),
        in_specs=[tok(RET_W_), tok(DIFF_W_), tok(d), modrow(2), modrow(3), modrow(4),
                  full(norm_g), full(wo_a), full(wo_b), full(wq_t), full(keys)],
        out_specs=(
            tok(d),
            pl.BlockSpec((d, tm), lambda i, j: (0, i * per + j)),
            pl.BlockSpec((n_hp, PEER_NKEYS_, tm), lambda i, j: (0, 0, i * per + j)),
        ),
        out_shape=(
            jax.ShapeDtypeStruct((b, l, d), F32),
            jax.ShapeDtypeStruct((d, t), BF16),
            jax.ShapeDtypeStruct((n_hp, PEER_NKEYS_, t), F32),
        ),
        compiler_params=_params(("parallel", "parallel")),
        name="mix_out",
    )(ret, dif, x, mod3, mod3, mod3, norm_g, wo_a, wo_b, wq_t, keys)


def _sort16_pairs():
    def merge(lo, hi, r):
        step = r * 2
        if step < hi - lo:
            yield from merge(lo, hi, step)
            yield from merge(lo + r, hi, step)
            for i in range(lo + r, hi - r, step):
                yield (i, i + r)
        else:
            yield (lo, lo + r)

    def sort(lo, hi):
        if hi - lo >= 1:
            mid = lo + (hi - lo) // 2
            yield from sort(lo, mid)
            yield from sort(mid + 1, hi)
            yield from merge(lo, hi, 1)

    return tuple(sort(0, 15))


SORT16_PAIRS_ = _sort16_pairs()


def _bitonic_merge_desc(c):
    c = list(c)
    d = len(c) // 2
    while d >= 1:
        for k in range(len(c)):
            if k & d == 0:
                hi = jnp.maximum(c[k], c[k + d])
                lo = jnp.minimum(c[k], c[k + d])
                c[k], c[k + d] = hi, lo
        d //= 2
    return c


def _merge_top(a, b):
    n = len(a)
    return _bitonic_merge_desc([jnp.maximum(a[k], b[n - 1 - k]) for k in range(n)])


def _merge_sublanes(a):
    for shift in (4, 2, 1):
        a = _merge_top(a, [pltpu.roll(v, shift, 0) for v in a])
    return a


def _top16_rows(x):
    n = PEER_NKEYS_ // SUBLANES_
    a = [x[v * SUBLANES_:(v + 1) * SUBLANES_, :] for v in range(n)]
    for i, j in SORT16_PAIRS_:
        hi = jnp.maximum(a[i], a[j])
        lo = jnp.minimum(a[i], a[j])
        a[i], a[j] = hi, lo
    return _merge_sublanes(a)


def _route_group(s1, s2):
    w = s1.shape[-1]
    k = PEER_TOPK_
    v1 = _top16_rows(s1)
    v2 = _top16_rows(s2)
    sub = lax.broadcasted_iota(jnp.int32, (SUBLANES_, w), 0)
    inf = jnp.float32(jnp.inf)

    def pack(vals):
        out = vals[SUBLANES_ - 1]
        for a in range(SUBLANES_ - 2, -1, -1):
            out = jnp.where(sub == a, vals[a], out)
        return out

    v1_lo = pack(v1[:SUBLANES_])
    v1_hi = pack(v1[SUBLANES_:])
    cand = [jnp.where(sub < k // (b + 1), v1_lo + v2[b], -inf) for b in range(k)]
    cand_hi = v1_hi + v2[0]
    merged = list(cand)
    merged[k - 1] = jnp.maximum(merged[k - 1], cand_hi)
    best = _merge_sublanes(_bitonic_merge_desc(merged))
    top = best[0]
    cut = best[k - 1]
    z = jnp.exp(best[0] - top)
    for b in range(1, k):
        z = z + jnp.exp(best[b] - top)
    rz = 0.5 / z

    n_lo = jnp.zeros((SUBLANES_, w), F32)
    for b in range(k):
        n_lo = n_lo + jnp.where(cand[b] >= cut, 1.0, 0.0)
    n_hi = jnp.where(cand_hi >= cut, 1.0, 0.0)
    n_rank = ([jnp.broadcast_to(n_lo[a:a + 1, :], (SUBLANES_, w)) for a in range(SUBLANES_)]
              + [jnp.broadcast_to(n_hi[a:a + 1, :], (SUBLANES_, w)) for a in range(SUBLANES_)])

    rank2, e2, count, c1 = [], [], [], []
    for v in range(PEER_NKEYS_ // SUBLANES_):
        rows = slice(v * SUBLANES_, (v + 1) * SUBLANES_)
        x1 = s1[rows, :]
        x2 = s2[rows, :]
        cnt = jnp.zeros((SUBLANES_, w), F32)
        for a in range(k):
            cnt = jnp.where(x1 == v1[a], n_rank[a], cnt)
        rk = jnp.full((SUBLANES_, w), float(k), F32)
        for b in range(k - 1, -1, -1):
            rk = jnp.where(x2 >= v2[b], float(b), rk)
        count.append(cnt)
        c1.append(jnp.exp(x1 - v1[0]) * rz)
        rank2.append(rk)
        e2.append(jnp.exp(x2 - v2[0]))
    cat = lambda parts: jnp.concatenate(parts, axis=0)
    return cat(rank2), cat(e2), cat(count), cat(c1)


def _route_kernel(s_ref, r2_ref, e2_ref, n_ref, c1_ref, *, group):
    tr = s_ref.shape[-1]
    packed = (PEER_NKEYS_ // (2 * SUBLANES_), 2 * SUBLANES_, group)

    def body(g, carry):
        cols = pl.ds(pl.multiple_of(g * group, group), group)
        rank2, e2, count, c1 = _route_group(s_ref[0, 0, :, cols], s_ref[0, 1, :, cols])
        r2_ref[0, :, :, cols] = rank2.astype(BF16).reshape(packed)
        e2_ref[0, :, :, cols] = e2.astype(BF16).reshape(packed)
        n_ref[0, :, cols] = count
        c1_ref[0, :, cols] = c1
        return carry

    lax.fori_loop(0, tr // group, body, 0)


def _route(st4, tr, group):
    heads, _, nk, t = st4.shape
    pk = (nk // (2 * SUBLANES_), 2 * SUBLANES_)
    out_pk = jax.ShapeDtypeStruct((heads,) + pk + (t,), BF16)
    out_f = jax.ShapeDtypeStruct((heads, nk, t), F32)
    spec_pk = pl.BlockSpec((1,) + pk + (tr,), lambda h, j: (h, 0, 0, j))
    spec_f = pl.BlockSpec((1, nk, tr), lambda h, j: (h, 0, j))
    return pl.pallas_call(
        functools.partial(_route_kernel, group=group),
        grid=(heads, t // tr),
        in_specs=[pl.BlockSpec((1, 2, nk, tr), lambda h, j: (h, 0, 0, j))],
        out_specs=(spec_pk, spec_pk, spec_f, spec_f),
        out_shape=(out_pk, out_pk, out_f, out_f),
        compiler_params=_params(("parallel", "parallel")),
        name="peer_route",
    )(st4)


def _peer_kernel(ht_ref, u_ref, vtp_ref, vt_ref, r2_ref, e2_ref, n_ref, c1_ref, x1_ref, gate_ref, o_ref,
                 acc_ref, g_scr, nb_scr, cb_scr, *, chunk, n_blocks):
    e = pl.program_id(1)
    te = u_ref.shape[0]
    tt = ht_ref.shape[1]
    pk = 2 * SUBLANES_
    rows_per_chunk = chunk // PEER_NKEYS_
    zero = jnp.zeros((), BF16)
    n_chunks = te // chunk
    groups = PEER_NKEYS_ // pk

    def fold(vt_chunk, c):
        acc_ref[...] += _dot(vt_chunk, g_scr[c].reshape(chunk, tt))

    def vt_cols(c):
        return vt_ref[:, c * chunk:(c + 1) * chunk]

    def weighted_activations(c, a):
        act = (a * (1.0 + lax.erf(a * (2.0 ** -0.5)))).astype(BF16)
        act = act.reshape(rows_per_chunk * groups, pk, tt)
        for r in range(rows_per_chunk):
            i = c * rows_per_chunk + r
            for h in range(PEER_HEADS_):
                nb_scr[h, i] = jnp.broadcast_to(n_ref[h, i:i + 1, :], (pk, tt)).astype(BF16)
                cb_scr[h, i] = jnp.broadcast_to(c1_ref[h, i:i + 1, :], (pk, tt)).astype(BF16)
            for lg in range(tt // PACKED_LANES_):
                cols = slice(lg * PACKED_LANES_, (lg + 1) * PACKED_LANES_)
                cnts = [nb_scr[h, i:i + 1, :, cols] for h in range(PEER_HEADS_)]
                c1s = [cb_scr[h, i:i + 1, :, cols] for h in range(PEER_HEADS_)]
                for s in range(groups):
                    w = None
                    for h in range(PEER_HEADS_):
                        term = jnp.where(r2_ref[h, s:s + 1, :, cols] < cnts[h],
                                         e2_ref[h, s:s + 1, :, cols], zero) * c1s[h]
                        w = term if w is None else w + term
                    g_scr[c, r * groups + s:r * groups + s + 1, :, cols] = (
                        w * act[r * groups + s:r * groups + s + 1, :, cols])

    def step(first):
        ht = ht_ref[...]
        first_dot = lambda c: _dot(u_ref[c * chunk:(c + 1) * chunk, :], ht)
        a_next = first_dot(0)
        for c in range(n_chunks):
            a = a_next
            if c + 1 < n_chunks:
                a_next = first_dot(c + 1)
            if c > 0:
                fold(vt_cols(c - 1), c - 1)
            elif not first:
                fold(vtp_ref[:, (n_chunks - 1) * chunk:], n_chunks - 1)
            weighted_activations(c, a)

    @pl.when(e == 0)
    def _():
        acc_ref[...] = jnp.zeros_like(acc_ref)
        step(True)

    @pl.when((e > 0) & (e < n_blocks))
    def _():
        step(False)

    @pl.when(e == n_blocks)
    def _():
        fold(vtp_ref[:, (n_chunks - 1) * chunk:], n_chunks - 1)
        o_ref[0] = x1_ref[0] + gate_ref[0] * acc_ref[...].T


def _peer(ht, u_bf, vt_bf, rank2, e2, count, c1, x1, mod3, tt, te):
    d, t = ht.shape
    n_exp = u_bf.shape[0]
    b, l, _ = x1.shape
    per = l // tt
    rows = te // PEER_NKEYS_
    pk = (PEER_NKEYS_ // (2 * SUBLANES_), 2 * SUBLANES_)
    chunk = 2 * PEER_NKEYS_
    n_blocks = n_exp // te
    g_scratch = pltpu.VMEM((te // chunk, chunk // pk[1], pk[1], tt), BF16)
    cur = lambda e: jnp.minimum(e, n_blocks - 1)
    return pl.pallas_call(
        functools.partial(_peer_kernel, chunk=chunk, n_blocks=n_blocks),
        grid=(t // tt, n_blocks + 1),
        in_specs=[
            pl.BlockSpec((d, tt), lambda i, e: (0, i)),
            pl.BlockSpec((te, d), lambda i, e: (cur(e), 0)),
            pl.BlockSpec((d, te), lambda i, e: (0, jnp.maximum(e - 1, 0))),
            pl.BlockSpec((d, te), lambda i, e: (0, cur(e))),
            pl.BlockSpec((PEER_HEADS_,) + pk + (tt,), lambda i, e: (0, 0, 0, i)),
            pl.BlockSpec((PEER_HEADS_,) + pk + (tt,), lambda i, e: (0, 0, 0, i)),
            pl.BlockSpec((PEER_HEADS_, rows, tt), lambda i, e: (0, cur(e), i)),
            pl.BlockSpec((PEER_HEADS_, rows, tt), lambda i, e: (0, cur(e), i)),
            pl.BlockSpec((1, tt, d), lambda i, e: (i // per, i % per, 0)),
            pl.BlockSpec((1, 1, d), lambda i, e: (i // per, 0, 5)),
        ],
        out_specs=pl.BlockSpec((1, tt, d), lambda i, e: (i // per, i % per, 0)),
        out_shape=jax.ShapeDtypeStruct((b, l, d), F32),
        scratch_shapes=[pltpu.VMEM((d, tt), F32), g_scratch,
                        pltpu.VMEM((PEER_HEADS_, rows, pk[1], tt), BF16),
                        pltpu.VMEM((PEER_HEADS_, rows, pk[1], tt), BF16)],
        compiler_params=_params(("parallel", "arbitrary")),
        name="peer_dense",
    )(ht, u_bf, vt_bf, vt_bf, rank2, e2, count, c1, x1, mod3)


def _rope_tables(seq):
    quarter = RET_DK_ // 4
    freqs = ROPE_BASE_ ** (-jnp.arange(quarter, dtype=F32) / quarter)
    rows = seq // GRID_W_
    row = jnp.repeat(jnp.arange(rows, dtype=F32), GRID_W_)
    col = jnp.tile(jnp.arange(GRID_W_, dtype=F32), rows)
    ar = row[:, None] * freqs
    ac = col[:, None] * freqs
    ang = jnp.concatenate([ar, ar, ac, ac], axis=-1)
    ang = jnp.concatenate([ang, ang], axis=-1)
    cos, sin = jnp.cos(ang), jnp.sin(ang)
    first = (jnp.arange(LANES_) % 32) < 16
    return cos, jnp.where(first, -sin, 0.0), jnp.where(first, 0.0, sin)


def _tiles(seq, ctx_len, tokens):
    pick = lambda n, pref: next(c for c in pref if n % c == 0)
    return dict(
        tm_in=pick(seq, (512, 256, 128)),
        tm_ctx=pick(ctx_len, (256, 128)),
        tq=pick(seq, (512, 256, 128)),
        tm_mix=pick(seq, (512, 256, 128)),
        tr=pick(tokens, (1024, 512, 256, 128)),
        route_group=pick(tokens, (256, 128)),
        tt=pick(seq, (1024, 512, 256)),
        te=1024,
    )


def kernel(x, c, ctx, c_ctx, w_mod, b_mod, norm1_g, norm2_g, w_in, ret_decay_logit, ret_norm_g,
           diff_qk_norm_g, diff_lambda, diff_norm_g, w_out, peer_w_query, peer_sub_keys, peer_u, peer_v):
    assert w_mod.shape[0] == 1, "single-layer stack"
    b, l, d = x.shape
    ctx_len = ctx.shape[1]
    t = b * l
    tiles = _tiles(l, ctx_len, t)

    n_rows = -(-(b + 1) // SUBLANES_) * SUBLANES_
    cond = jnp.zeros((n_rows, d), F32).at[:b].set(c).at[b].set(c_ctx)
    mod = _adaln(cond, w_mod[0], b_mod[0])
    mod3 = mod.reshape(n_rows, 1, N_MOD_ * d)

    w_in_bf = w_in[0].astype(BF16)
    g1 = norm1_g[0].reshape(1, d)
    g2 = norm2_g[0].reshape(1, d)
    qkg = jnp.tile(diff_qk_norm_g[0], (1, DIFF_W_ // DIFF_D_))
    tabs = _rope_tables(l)
    rq, rk, rv, rg, dq, dk, dv = _inproj(x, mod3, lambda i: i, g1, w_in_bf, tabs, qkg, True,
                                         tiles["tm_in"])
    ctabs = tuple(jnp.zeros((ctx_len, LANES_), F32) for _ in range(3))
    _, crk, crv, _, _, cdk, cdv = _inproj(ctx, mod3, lambda i: b, g1, w_in_bf, ctabs, qkg, False,
                                          tiles["tm_ctx"])

    decay_b = jnp.broadcast_to(ret_decay_logit[0].T[:, :, None], (RET_HEADS_, 2, LANES_))
    ret = _retention(rq, rk, rv, rg, crk, crv, decay_b, ret_norm_g[0].reshape(RET_HEADS_, 1, RET_DV_))
    dif = _diffattn(dq, dk, dv, cdk, cdv, diff_lambda[0],
                    diff_norm_g[0].reshape(DIFF_HEADS_, 1, DIFF_DV_), tiles["tq"])

    wo = w_out[0].astype(BF16)
    wq_t = peer_w_query[0].T.astype(BF16)
    keys = peer_sub_keys[0].reshape(2 * PEER_HEADS_, PEER_NKEYS_, PEER_HALF_).astype(BF16)
    x1, ht, st = _mixout(ret, dif, x, mod3, g2, wo[:RET_W_], wo[RET_W_:], wq_t, keys, tiles["tm_mix"])

    st4 = st.reshape(PEER_HEADS_, 2, PEER_NKEYS_, t)
    rank2, e2, count, c1 = _route(st4, tiles["tr"], tiles["route_group"])

    u_bf = peer_u[0].astype(BF16)
    vt_bf = peer_v[0].T.astype(BF16)
    return _peer(ht, u_bf, vt_bf, rank2, e2, count, c1, x1, mod3, tiles["tt"], tiles["te"])
```

```python
import functools
import math

import jax
import jax.numpy as jnp
from jax import lax
from jax.experimental import pallas as pl
from jax.experimental.pallas import tpu as pltpu

F32 = jnp.float32
BF16 = jnp.bfloat16

D_MODEL_ = 1024
N_MOD_ = 6
NORM_EPS_ = 1e-6
RET_HEADS_ = 4
RET_DK_ = 64
RET_DV_ = 128
RET_CHUNK_ = 128
DIFF_HEADS_ = 4
DIFF_D_ = 64
DIFF_DV_ = 128
GRID_W_ = 64
ROPE_BASE_ = 10000.0
PEER_HEADS_ = 8
PEER_NKEYS_ = 128
PEER_HALF_ = 128
PEER_TOPK_ = 16
LAM_INIT_ = 0.8 - 0.6 * math.exp(-0.3 * 0)

LANES_ = 128
SUBLANES_ = 8
PACKED_LANES_ = 2 * LANES_
VMEM_LIMIT_BYTES_ = 56 * 1024 * 1024

DIFF_Q_SCALE_ = math.log2(math.e) * DIFF_D_ ** -0.5

RET_QK_ = RET_HEADS_ * RET_DK_
RET_W_ = RET_HEADS_ * RET_DV_
DIFF_W_ = DIFF_HEADS_ * DIFF_DV_
IN_COLS_ = 2 * RET_QK_ + 2 * RET_W_ + 3 * DIFF_W_


def _dot(a, b):
    return jnp.dot(a, b, preferred_element_type=F32)


def _dot_nt(a, b):
    return lax.dot_general(a, b, (((1,), (1,)), ((), ())), preferred_element_type=F32)


def _sigmoid(x):
    return 1.0 / (1.0 + jnp.exp(-x))


def _params(sem, flags=None):
    return pltpu.CompilerParams(dimension_semantics=sem, vmem_limit_bytes=VMEM_LIMIT_BYTES_, flags=flags)


def _adaln_kernel(cond_ref, w_ref, b_ref, o_ref):
    a = cond_ref[...]
    a = a * _sigmoid(a)
    o_ref[...] = _dot(a.astype(BF16), w_ref[...].astype(BF16)) + b_ref[...]


def _adaln(cond, w_mod, b_mod):
    rows, d = cond.shape
    n = w_mod.shape[1]
    tn = d
    return pl.pallas_call(
        _adaln_kernel,
        grid=(n // tn,),
        in_specs=[
            pl.BlockSpec((rows, d), lambda j: (0, 0)),
            pl.BlockSpec((d, tn), lambda j: (0, j)),
            pl.BlockSpec((1, tn), lambda j: (0, j)),
        ],
        out_specs=pl.BlockSpec((rows, tn), lambda j: (0, j)),
        out_shape=jax.ShapeDtypeStruct((rows, n), F32),
        compiler_params=_params(("parallel",)),
        name="adaln",
    )(cond, w_mod, b_mod.reshape(1, n))


def _rope_slab(v, cos, sin_up, sin_dn):
    up = pltpu.roll(v, LANES_ - 16, 1)
    dn = pltpu.roll(v, 16, 1)
    return v * cos + up * sin_up + dn * sin_dn


def _group_rms_scale(v, eps):
    w = v.shape[-1]
    sel = (lax.broadcasted_iota(jnp.int32, (w, LANES_), 0) // DIFF_D_
           == lax.broadcasted_iota(jnp.int32, (w, LANES_), 1)).astype(BF16)
    expand = (lax.broadcasted_iota(jnp.int32, (LANES_, w), 1) // DIFF_D_
              == lax.broadcasted_iota(jnp.int32, (LANES_, w), 0)).astype(BF16)
    v2 = v * v
    hi = v2.astype(BF16)
    lo = (v2 - hi.astype(F32)).astype(BF16)
    ssum = _dot(hi, sel) + _dot(lo, sel)
    r = lax.rsqrt(ssum * (1.0 / DIFF_D_) + eps)
    rhi = r.astype(BF16)
    rlo = (r - rhi.astype(F32)).astype(BF16)
    return _dot(rhi, expand) + _dot(rlo, expand)


def _inproj_kernel(x_ref, sh_ref, sc_ref, g_ref, w_ref, cos_ref, su_ref, sd_ref, qkg_ref,
                   rq_ref, rk_ref, rv_ref, rg_ref, dq_ref, dk_ref, dv_ref, y_scr, *, rope):
    x = x_ref[0]
    ms = jnp.mean(x * x, axis=-1, keepdims=True)
    h = x * lax.rsqrt(ms + NORM_EPS_) * g_ref[...]
    h = h * (1.0 + sc_ref[0]) + sh_ref[0]
    y_scr[...] = _dot(h.astype(BF16), w_ref[...])

    cos = cos_ref[...]
    su = su_ref[...]
    sd = sd_ref[...]

    def put(dst_ref, src_col, width, scale, norm_g):
        v = y_scr[:, src_col:src_col + width]
        if norm_g is not None:
            v = v * _group_rms_scale(v, NORM_EPS_) * norm_g
        for s in range(width // LANES_):
            slab = v[:, s * LANES_:(s + 1) * LANES_]
            if rope:
                slab = _rope_slab(slab, cos, su, sd)
            if scale != 1.0:
                slab = slab * scale
            dst_ref[0, :, s * LANES_:(s + 1) * LANES_] = slab.astype(dst_ref.dtype)

    c = 0
    put(rq_ref, c, RET_QK_, 1.0, None)
    c += RET_QK_
    put(rk_ref, c, RET_QK_, RET_DK_ ** -0.5, None)
    c += RET_QK_
    rv_ref[0] = y_scr[:, c:c + RET_W_].astype(BF16)
    c += RET_W_
    rg_ref[0] = y_scr[:, c:c + RET_W_]
    c += RET_W_
    put(dq_ref, c, DIFF_W_, DIFF_Q_SCALE_, qkg_ref[0:1, :])
    c += DIFF_W_
    put(dk_ref, c, DIFF_W_, 1.0, qkg_ref[1:2, :])
    c += DIFF_W_
    dv_ref[0] = y_scr[:, c:c + DIFF_W_].astype(BF16)


def _inproj(xs, mod3, mod_row, norm_g, w_in_bf, tabs, qkg, rope, tm):
    b, l, d = xs.shape
    cos, su, sd = tabs
    grid = (b, l // tm)
    tok = lambda w: pl.BlockSpec((1, tm, w), lambda i, j: (i, j, 0))
    tab = pl.BlockSpec((tm, LANES_), lambda i, j: (j, 0))
    full = lambda a: pl.BlockSpec(a.shape, lambda i, j: (0,) * a.ndim)
    out_shapes = (
        jax.ShapeDtypeStruct((b, l, RET_QK_), BF16),
        jax.ShapeDtypeStruct((b, l, RET_QK_), BF16),
        jax.ShapeDtypeStruct((b, l, RET_W_), BF16),
        jax.ShapeDtypeStruct((b, l, RET_W_), F32),
        jax.ShapeDtypeStruct((b, l, DIFF_W_), BF16),
        jax.ShapeDtypeStruct((b, l, DIFF_W_), BF16),
        jax.ShapeDtypeStruct((b, l, DIFF_W_), BF16),
    )
    return pl.pallas_call(
        functools.partial(_inproj_kernel, rope=rope),
        grid=grid,
        in_specs=[
            tok(d),
            pl.BlockSpec((1, 1, d), lambda i, j: (mod_row(i), 0, 0)),
            pl.BlockSpec((1, 1, d), lambda i, j: (mod_row(i), 0, 1)),
            full(norm_g),
            full(w_in_bf),
            tab, tab, tab,
            full(qkg),
        ],
        out_specs=(tok(RET_QK_), tok(RET_QK_), tok(RET_W_), tok(RET_W_),
                   tok(DIFF_W_), tok(DIFF_W_), tok(DIFF_W_)),
        out_shape=out_shapes,
        scratch_shapes=[pltpu.VMEM((tm, IN_COLS_), F32)],
        compiler_params=_params(("parallel", "parallel")),
        name="inproj_rope" if rope else "inproj_ctx",
    )(xs, mod3, mod3, norm_g, w_in_bf, cos, su, sd, qkg)


def _log_sigmoid(x):
    return jnp.minimum(x, 0.0) - jnp.log(1.0 + jnp.exp(-jnp.abs(x)))


def _retention_kernel(dl_ref, q_ref, k_ref, v_ref, rg_ref, ck_ref, cv_ref, g_ref, o_ref,
                      lf_scr, lb_scr, rf_scr, rb_scr, *, seq, ctx_len):
    c = RET_CHUNK_
    n_chunks = seq // c
    head = pl.program_id(1)
    lane = lax.broadcasted_iota(jnp.int32, (1, LANES_), 1)
    head_lanes = (lane // RET_DK_) == (head % 2)

    dl = dl_ref[0]
    lg_f = _log_sigmoid(dl[0:1, :])
    lg_b = _log_sigmoid(dl[1:2, :])

    row = lax.broadcasted_iota(jnp.int32, (c, LANES_), 0).astype(F32)
    col = lax.broadcasted_iota(jnp.int32, (c, LANES_), 1).astype(F32)
    kw_f = jnp.exp((c - 1.0 - row) * lg_f)
    kw_b = jnp.exp(row * lg_b)
    qw_f = jnp.exp((row + 1.0) * lg_f)
    qw_b = jnp.exp((c - row) * lg_b)
    dist = row - col
    decay = jnp.where(dist >= 0.0,
                      jnp.exp(jnp.maximum(dist, 0.0) * lg_f),
                      jnp.exp(jnp.maximum(-dist, 0.0) * lg_b))
    chunk_f = jnp.exp(float(c) * lg_f)
    chunk_b = jnp.exp(float(c) * lg_b)

    crow = lax.broadcasted_iota(jnp.int32, (ctx_len, LANES_), 0).astype(F32)
    ck = ck_ref[0].astype(F32)
    cv = cv_ref[0]
    s_f = _dot((ck * jnp.exp((ctx_len - 1.0 - crow) * lg_f)).T.astype(BF16), cv)
    s_b = _dot((ck * jnp.exp(crow * lg_b)).T.astype(BF16), cv)

    def local_states(n, carry):
        sl = pl.ds(pl.multiple_of(n * c, c), c)
        kn = k_ref[0, sl, :].astype(F32)
        vn = v_ref[0, sl, :]
        lf_scr[n] = _dot((kn * kw_f).T.astype(BF16), vn)
        lb_scr[n] = _dot((kn * kw_b).T.astype(BF16), vn)
        return carry

    lax.fori_loop(0, n_chunks, local_states, 0, unroll=True)

    def scan_f(n, r):
        rf_scr[n] = r
        return chunk_f * r + lf_scr[n]

    lax.fori_loop(0, n_chunks, scan_f, s_f)

    def scan_b(m, r):
        n = n_chunks - 1 - m
        rb_scr[n] = r
        return chunk_b * r + lb_scr[n]

    lax.fori_loop(0, n_chunks, scan_b, s_b)

    gain = g_ref[0]

    def outputs(n, carry):
        sl = pl.ds(pl.multiple_of(n * c, c), c)
        qn = jnp.where(head_lanes, q_ref[0, sl, :], jnp.zeros((), BF16))
        scores = _dot_nt(qn, k_ref[0, sl, :])
        y = _dot((scores * decay).astype(BF16), v_ref[0, sl, :])
        y = y + qw_f * _dot(qn, rf_scr[n].astype(BF16)) + qw_b * _dot(qn, rb_scr[n].astype(BF16))
        ms = jnp.mean(y * y, axis=-1, keepdims=True)
        yn = y * lax.rsqrt(ms + NORM_EPS_) * gain
        rg = rg_ref[0, sl, :]
        o_ref[0, sl, :] = (yn * (rg * _sigmoid(rg))).astype(o_ref.dtype)
        return carry

    lax.fori_loop(0, n_chunks, outputs, 0, unroll=True)


def _retention(rq, rk, rv, rg, crk, crv, decay_b, norm_g3):
    b, l, _ = rq.shape
    ctx_len = crk.shape[1]
    n_chunks = l // RET_CHUNK_
    pair = lambda length: pl.BlockSpec((1, length, LANES_), lambda i, h: (i, 0, h // 2))
    own = lambda length: pl.BlockSpec((1, length, LANES_), lambda i, h: (i, 0, h))
    state = pltpu.VMEM((n_chunks, LANES_, RET_DV_), F32)
    return pl.pallas_call(
        functools.partial(_retention_kernel, seq=l, ctx_len=ctx_len),
        grid=(b, RET_HEADS_),
        in_specs=[
            pl.BlockSpec((1, 2, LANES_), lambda i, h: (h, 0, 0)),
            pair(l), pair(l), own(l), own(l), pair(ctx_len), own(ctx_len),
            pl.BlockSpec((1, 1, RET_DV_), lambda i, h: (h, 0, 0)),
        ],
        out_specs=own(l),
        out_shape=jax.ShapeDtypeStruct((b, l, RET_W_), BF16),
        scratch_shapes=[state, state, state, state],
        compiler_params=_params(("parallel", "parallel")),
        name="retention",
    )(decay_b, rq, rk, rv, rg, crk, crv, norm_g3)


def _diffattn_kernel(lam_ref, q_ref, k_ref, v_ref, ck_ref, cv_ref, g_ref, o_ref):
    q = q_ref[0]
    lane = lax.broadcasted_iota(jnp.int32, (1, LANES_), 1)
    first = lane < DIFF_D_
    zero = jnp.zeros((), BF16)
    q1 = jnp.where(first, q, zero)
    q2 = jnp.where(first, zero, q)
    lp = lam_ref[...]
    lam = (jnp.exp(jnp.sum(lp[0:1] * lp[1:2], axis=-1, keepdims=True))
           - jnp.exp(jnp.sum(lp[2:3] * lp[3:4], axis=-1, keepdims=True)) + LAM_INIT_)

    def attend(qh):
        s_c = _dot_nt(qh, ck_ref[0])
        m_c = jnp.max(s_c, axis=-1, keepdims=True)
        p_c = jnp.exp2(s_c - m_c)
        l_c = jnp.sum(p_c, axis=-1, keepdims=True)
        a_c = _dot(p_c.astype(BF16), cv_ref[0])
        s_l = _dot_nt(qh, k_ref[0])
        m = jnp.maximum(m_c, jnp.max(s_l, axis=-1, keepdims=True))
        alpha = jnp.exp2(m_c - m)
        p_l = jnp.exp2(s_l - m)
        denom = alpha * l_c + jnp.sum(p_l, axis=-1, keepdims=True)
        return (alpha * a_c + _dot(p_l.astype(BF16), v_ref[0])) / denom

    y = attend(q1) - lam * attend(q2)
    ms = jnp.mean(y * y, axis=-1, keepdims=True)
    yn = y * lax.rsqrt(ms + NORM_EPS_) * g_ref[0]
    o_ref[0] = (yn * (1.0 - LAM_INIT_)).astype(o_ref.dtype)


def _diffattn(dq, dk, dv, cdk, cdv, lam_params, norm_g3, tq):
    b, l, _ = dq.shape
    ctx_len = cdk.shape[1]
    kv = lambda length: pl.BlockSpec((1, length, LANES_), lambda i, h, j: (i, 0, h))
    qo = pl.BlockSpec((1, tq, LANES_), lambda i, h, j: (i, j, h))
    return pl.pallas_call(
        _diffattn_kernel,
        grid=(b, DIFF_HEADS_, l // tq),
        in_specs=[
            pl.BlockSpec(lam_params.shape, lambda i, h, j: (0, 0)),
            qo, kv(l), kv(l), kv(ctx_len), kv(ctx_len),
            pl.BlockSpec((1, 1, DIFF_DV_), lambda i, h, j: (h, 0, 0)),
        ],
        out_specs=qo,
        out_shape=jax.ShapeDtypeStruct((b, l, DIFF_W_), BF16),
        compiler_params=_params(("parallel", "parallel", "parallel")),
        name="diffattn",
    )(lam_params, dq, dk, dv, cdk, cdv, norm_g3)


def _mixout_kernel(ret_ref, dif_ref, x_ref, gate_ref, sh_ref, sc_ref, g_ref, woa_ref, wob_ref,
                   wqt_ref, keys_ref, x1_ref, ht_ref, st_ref):
    o = _dot(ret_ref[0], woa_ref[...]) + _dot(dif_ref[0], wob_ref[...])
    x1 = x_ref[0] + gate_ref[0] * o
    x1_ref[0] = x1
    ms = jnp.mean(x1 * x1, axis=-1, keepdims=True)
    h2 = x1 * lax.rsqrt(ms + NORM_EPS_) * g_ref[...]
    h2 = h2 * (1.0 + sc_ref[0]) + sh_ref[0]
    ht = h2.T.astype(BF16)
    ht_ref[...] = ht
    qt = _dot(wqt_ref[...], ht).astype(BF16)
    for hp in range(2 * PEER_HEADS_):
        st_ref[hp] = _dot(keys_ref[hp], qt[hp * PEER_HALF_:(hp + 1) * PEER_HALF_, :])


def _mixout(ret, dif, x, mod3, norm_g, wo_a, wo_b, wq_t, keys, tm):
    b, l, d = x.shape
    t = b * l
    per = l // tm
    tok = lambda w: pl.BlockSpec((1, tm, w), lambda i, j: (i, j, 0))
    modrow = lambda k: pl.BlockSpec((1, 1, d), lambda i, j: (i, 0, k))
    full = lambda a: pl.BlockSpec(a.shape, lambda i, j: (0,) * a.ndim)
    n_hp = 2 * PEER_HEADS_
    return pl.pallas_call(
        _mixout_kernel,
        grid=(b, per),
        in_specs=[tok(RET_W_), tok(DIFF_W_), tok(d), modrow(2), modrow(3), modrow(4),
                  full(norm_g), full(wo_a), full(wo_b), full(wq_t), full(keys)],
        out_specs=(
            tok(d),
            pl.BlockSpec((d, tm), lambda i, j: (0, i * per + j)),
            pl.BlockSpec((n_hp, PEER_NKEYS_, tm), lambda i, j: (0, 0, i * per + j)),
        ),
        out_shape=(
            jax.ShapeDtypeStruct((b, l, d), F32),
            jax.ShapeDtypeStruct((d, t), BF16),
            jax.ShapeDtypeStruct((n_hp, PEER_NKEYS_, t), F32),
        ),
        compiler_params=_params(("parallel", "parallel")),
        name="mix_out",
    )(ret, dif, x, mod3, mod3, mod3, norm_g, wo_a, wo_b, wq_t, keys)


def _sort16_pairs():
    def merge(lo, hi, r):
        step = r * 2
        if step < hi - lo:
            yield from merge(lo, hi, step)
            yield from merge(lo + r, hi, step)
            for i in range(lo + r, hi - r, step):
                yield (i, i + r)
        else:
            yield (lo, lo + r)

    def sort(lo, hi):
        if hi - lo >= 1:
            mid = lo + (hi - lo) // 2
            yield from sort(lo, mid)
            yield from sort(mid + 1, hi)
            yield from merge(lo, hi, 1)

    return tuple(sort(0, 15))


SORT16_PAIRS_ = _sort16_pairs()


def _bitonic_merge_desc(c):
    c = list(c)
    d = len(c) // 2
    while d >= 1:
        for k in range(len(c)):
            if k & d == 0:
                hi = jnp.maximum(c[k], c[k + d])
                lo = jnp.minimum(c[k], c[k + d])
                c[k], c[k + d] = hi, lo
        d //= 2
    return c


def _merge_top(a, b):
    n = len(a)
    return _bitonic_merge_desc([jnp.maximum(a[k], b[n - 1 - k]) for k in range(n)])


def _merge_sublanes(a):
    for shift in (4, 2, 1):
        a = _merge_top(a, [pltpu.roll(v, shift, 0) for v in a])
    return a


def _top16_rows(x):
    n = PEER_NKEYS_ // SUBLANES_
    a = [x[v * SUBLANES_:(v + 1) * SUBLANES_, :] for v in range(n)]
    for i, j in SORT16_PAIRS_:
        hi = jnp.maximum(a[i], a[j])
        lo = jnp.minimum(a[i], a[j])
        a[i], a[j] = hi, lo
    return _merge_sublanes(a)


def _route_group(s1, s2):
    w = s1.shape[-1]
    k = PEER_TOPK_
    v1 = _top16_rows(s1)
    v2 = _top16_rows(s2)
    sub = lax.broadcasted_iota(jnp.int32, (SUBLANES_, w), 0)
    inf = jnp.float32(jnp.inf)

    def pack(vals):
        out = vals[SUBLANES_ - 1]
        for a in range(SUBLANES_ - 2, -1, -1):
            out = jnp.where(sub == a, vals[a], out)
        return out

    v1_lo = pack(v1[:SUBLANES_])
    v1_hi = pack(v1[SUBLANES_:])
    cand = [jnp.where(sub < k // (b + 1), v1_lo + v2[b], -inf) for b in range(k)]
    cand_hi = v1_hi + v2[0]
    merged = list(cand)
    merged[k - 1] = jnp.maximum(merged[k - 1], cand_hi)
    best = _merge_sublanes(_bitonic_merge_desc(merged))
    top = best[0]
    cut = best[k - 1]
    z = jnp.exp(best[0] - top)
    for b in range(1, k):
        z = z + jnp.exp(best[b] - top)
    rz = 0.5 / z

    n_lo = jnp.zeros((SUBLANES_, w), F32)
    for b in range(k):
        n_lo = n_lo + jnp.where(cand[b] >= cut, 1.0, 0.0)
    n_hi = jnp.where(cand_hi >= cut, 1.0, 0.0)
    n_rank = ([jnp.broadcast_to(n_lo[a:a + 1, :], (SUBLANES_, w)) for a in range(SUBLANES_)]
              + [jnp.broadcast_to(n_hi[a:a + 1, :], (SUBLANES_, w)) for a in range(SUBLANES_)])

    rank2, e2, count, c1 = [], [], [], []
    for v in range(PEER_NKEYS_ // SUBLANES_):
        rows = slice(v * SUBLANES_, (v + 1) * SUBLANES_)
        x1 = s1[rows, :]
        x2 = s2[rows, :]
        cnt = jnp.zeros((SUBLANES_, w), F32)
        for a in range(k):
            cnt = jnp.where(x1 == v1[a], n_rank[a], cnt)
        rk = jnp.full((SUBLANES_, w), float(k), F32)
        for b in range(k - 1, -1, -1):
            rk = jnp.where(x2 >= v2[b], float(b), rk)
        count.append(cnt)
        c1.append(jnp.exp(x1 - v1[0]) * rz)
        rank2.append(rk)
        e2.append(jnp.exp(x2 - v2[0]))
    cat = lambda parts: jnp.concatenate(parts, axis=0)
    return cat(rank2), cat(e2), cat(count), cat(c1)


def _route_kernel(s_ref, r2_ref, e2_ref, n_ref, c1_ref, *, group):
    tr = s_ref.shape[-1]
    packed = (PEER_NKEYS_ // (2 * SUBLANES_), 2 * SUBLANES_, group)

    def body(g, carry):
        cols = pl.ds(pl.multiple_of(g * group, group), group)
        rank2, e2, count, c1 = _route_group(s_ref[0, 0, :, cols], s_ref[0, 1, :, cols])
        r2_ref[0, :, :, cols] = rank2.astype(BF16).reshape(packed)
        e2_ref[0, :, :, cols] = e2.astype(BF16).reshape(packed)
        n_ref[0, :, cols] = count
        c1_ref[0, :, cols] = c1
        return carry

    lax.fori_loop(0, tr // group, body, 0)


def _route(st4, tr, group):
    heads, _, nk, t = st4.shape
    pk = (nk // (2 * SUBLANES_), 2 * SUBLANES_)
    out_pk = jax.ShapeDtypeStruct((heads,) + pk + (t,), BF16)
    out_f = jax.ShapeDtypeStruct((heads, nk, t), F32)
    spec_pk = pl.BlockSpec((1,) + pk + (tr,), lambda h, j: (h, 0, 0, j))
    spec_f = pl.BlockSpec((1, nk, tr), lambda h, j: (h, 0, j))
    return pl.pallas_call(
        functools.partial(_route_kernel, group=group),
        grid=(heads, t // tr),
        in_specs=[pl.BlockSpec((1, 2, nk, tr), lambda h, j: (h, 0, 0, j))],
        out_specs=(spec_pk, spec_pk, spec_f, spec_f),
        out_shape=(out_pk, out_pk, out_f, out_f),
        compiler_params=_params(("parallel", "parallel")),
        name="peer_route",
    )(st4)


def _peer_kernel(ht_ref, u_ref, vtp_ref, vt_ref, r2_ref, e2_ref, n_ref, c1_ref, x1_ref, gate_ref, o_ref,
                 acc_ref, g_scr, *, chunk, n_blocks):
    e = pl.program_id(1)
    te = u_ref.shape[0]
    tt = ht_ref.shape[1]
    pk = 2 * SUBLANES_
    rows_per_chunk = chunk // PEER_NKEYS_
    zero = jnp.zeros((), BF16)
    n_chunks = te // chunk
    groups = PEER_NKEYS_ // pk

    def fold(vt_chunk, c):
        acc_ref[...] += _dot(vt_chunk, g_scr[c].reshape(chunk, tt))

    def vt_cols(c):
        return vt_ref[:, c * chunk:(c + 1) * chunk]

    def weighted_activations(c, a):
        act = (a * (1.0 + lax.erf(a * (2.0 ** -0.5)))).astype(BF16)
        act = act.reshape(rows_per_chunk * groups, pk, tt)
        for r in range(rows_per_chunk):
            i = c * rows_per_chunk + r
            spread = lambda ref, h: jnp.broadcast_to(ref[h, i:i + 1, :], (pk, tt)).astype(BF16)[None]
            cnt_rows = [spread(n_ref, h) for h in range(PEER_HEADS_)]
            c1_rows = [spread(c1_ref, h) for h in range(PEER_HEADS_)]
            for lg in range(tt // PACKED_LANES_):
                cols = slice(lg * PACKED_LANES_, (lg + 1) * PACKED_LANES_)
                cnts = [v[:, :, cols] for v in cnt_rows]
                c1s = [v[:, :, cols] for v in c1_rows]
                for s in range(groups):
                    w = None
                    for h in range(PEER_HEADS_):
                        term = jnp.where(r2_ref[h, s:s + 1, :, cols] < cnts[h],
                                         e2_ref[h, s:s + 1, :, cols], zero) * c1s[h]
                        w = term if w is None else w + term
                    g_scr[c, r * groups + s:r * groups + s + 1, :, cols] = (
                        w * act[r * groups + s:r * groups + s + 1, :, cols])

    def step(first):
        ht = ht_ref[...]
        first_dot = lambda c: _dot(u_ref[c * chunk:(c + 1) * chunk, :], ht)
        a_next = first_dot(0)
        for c in range(n_chunks):
            a = a_next
            if c + 1 < n_chunks:
                a_next = first_dot(c + 1)
            if c > 0:
                fold(vt_cols(c - 1), c - 1)
            elif not first:
                fold(vtp_ref[:, (n_chunks - 1) * chunk:], n_chunks - 1)
            weighted_activations(c, a)

    @pl.when(e == 0)
    def _():
        acc_ref[...] = jnp.zeros_like(acc_ref)
        step(True)

    @pl.when((e > 0) & (e < n_blocks))
    def _():
        step(False)

    @pl.when(e == n_blocks)
    def _():
        fold(vtp_ref[:, (n_chunks - 1) * chunk:], n_chunks - 1)
        o_ref[0] = x1_ref[0] + gate_ref[0] * acc_ref[...].T


def _peer(ht, u_bf, vt_bf, rank2, e2, count, c1, x1, mod3, tt, te):
    d, t = ht.shape
    n_exp = u_bf.shape[0]
    b, l, _ = x1.shape
    per = l // tt
    rows = te // PEER_NKEYS_
    pk = (PEER_NKEYS_ // (2 * SUBLANES_), 2 * SUBLANES_)
    chunk = 2 * PEER_NKEYS_
    n_blocks = n_exp // te
    g_scratch = pltpu.VMEM((te // chunk, chunk // pk[1], pk[1], tt), BF16)
    cur = lambda e: jnp.minimum(e, n_blocks - 1)
    return pl.pallas_call(
        functools.partial(_peer_kernel, chunk=chunk, n_blocks=n_blocks),
        grid=(t // tt, n_blocks + 1),
        in_specs=[
            pl.BlockSpec((d, tt), lambda i, e: (0, i)),
            pl.BlockSpec((te, d), lambda i, e: (cur(e), 0)),
            pl.BlockSpec((d, te), lambda i, e: (0, jnp.maximum(e - 1, 0))),
            pl.BlockSpec((d, te), lambda i, e: (0, cur(e))),
            pl.BlockSpec((PEER_HEADS_,) + pk + (tt,), lambda i, e: (0, 0, 0, i)),
            pl.BlockSpec((PEER_HEADS_,) + pk + (tt,), lambda i, e: (0, 0, 0, i)),
            pl.BlockSpec((PEER_HEADS_, rows, tt), lambda i, e: (0, cur(e), i)),
            pl.BlockSpec((PEER_HEADS_, rows, tt), lambda i, e: (0, cur(e), i)),
            pl.BlockSpec((1, tt, d), lambda i, e: (i // per, i % per, 0)),
            pl.BlockSpec((1, 1, d), lambda i, e: (i // per, 0, 5)),
        ],
        out_specs=pl.BlockSpec((1, tt, d), lambda i, e: (i // per, i % per, 0)),
        out_shape=jax.ShapeDtypeStruct((b, l, d), F32),
        scratch_shapes=[pltpu.VMEM((d, tt), F32), g_scratch],
        compiler_params=_params(("parallel", "arbitrary")),
        name="peer_dense",
    )(ht, u_bf, vt_bf, vt_bf, rank2, e2, count, c1, x1, mod3)


def _rope_tables(seq):
    quarter = RET_DK_ // 4
    freqs = ROPE_BASE_ ** (-jnp.arange(quarter, dtype=F32) / quarter)
    rows = seq // GRID_W_
    row = jnp.repeat(jnp.arange(rows, dtype=F32), GRID_W_)
    col = jnp.tile(jnp.arange(GRID_W_, dtype=F32), rows)
    ar = row[:, None] * freqs
    ac = col[:, None] * freqs
    ang = jnp.concatenate([ar, ar, ac, ac], axis=-1)
    ang = jnp.concatenate([ang, ang], axis=-1)
    cos, sin = jnp.cos(ang), jnp.sin(ang)
    first = (jnp.arange(LANES_) % 32) < 16
    return cos, jnp.where(first, -sin, 0.0), jnp.where(first, 0.0, sin)


def _tiles(seq, ctx_len, tokens):
    pick = lambda n, pref: next(c for c in pref if n % c == 0)
    return dict(
        tm_in=pick(seq, (512, 256, 128)),
        tm_ctx=pick(ctx_len, (256, 128)),
        tq=pick(seq, (512, 256, 128)),
        tm_mix=pick(seq, (512, 256, 128)),
        tr=pick(tokens, (1024, 512, 256, 128)),
        route_group=pick(tokens, (256, 128)),
        tt=pick(seq, (1024, 512, 256)),
        te=1024,
    )


def kernel(x, c, ctx, c_ctx, w_mod, b_mod, norm1_g, norm2_g, w_in, ret_decay_logit, ret_norm_g,
           diff_qk_norm_g, diff_lambda, diff_norm_g, w_out, peer_w_query, peer_sub_keys, peer_u, peer_v):
    assert w_mod.shape[0] == 1, "single-layer stack"
    b, l, d = x.shape
    ctx_len = ctx.shape[1]
    t = b * l
    tiles = _tiles(l, ctx_len, t)

    n_rows = -(-(b + 1) // SUBLANES_) * SUBLANES_
    cond = jnp.zeros((n_rows, d), F32).at[:b].set(c).at[b].set(c_ctx)
    mod = _adaln(cond, w_mod[0], b_mod[0])
    mod3 = mod.reshape(n_rows, 1, N_MOD_ * d)

    w_in_bf = w_in[0].astype(BF16)
    g1 = norm1_g[0].reshape(1, d)
    g2 = norm2_g[0].reshape(1, d)
    qkg = jnp.tile(diff_qk_norm_g[0], (1, DIFF_W_ // DIFF_D_))
    tabs = _rope_tables(l)
    rq, rk, rv, rg, dq, dk, dv = _inproj(x, mod3, lambda i: i, g1, w_in_bf, tabs, qkg, True,
                                         tiles["tm_in"])
    ctabs = tuple(jnp.zeros((ctx_len, LANES_), F32) for _ in range(3))
    _, crk, crv, _, _, cdk, cdv = _inproj(ctx, mod3, lambda i: b, g1, w_in_bf, ctabs, qkg, False,
                                          tiles["tm_ctx"])

    decay_b = jnp.broadcast_to(ret_decay_logit[0].T[:, :, None], (RET_HEADS_, 2, LANES_))
    ret = _retention(rq, rk, rv, rg, crk, crv, decay_b, ret_norm_g[0].reshape(RET_HEADS_, 1, RET_DV_))
    dif = _diffattn(dq, dk, dv, cdk, cdv, diff_lambda[0],
                    diff_norm_g[0].reshape(DIFF_HEADS_, 1, DIFF_DV_), tiles["tq"])

    wo = w_out[0].astype(BF16)
    wq_t = peer_w_query[0].T.astype(BF16)
    keys = peer_sub_keys[0].reshape(2 * PEER_HEADS_, PEER_NKEYS_, PEER_HALF_).astype(BF16)
    x1, ht, st = _mixout(ret, dif, x, mod3, g2, wo[:RET_W_], wo[RET_W_:], wq_t, keys, tiles["tm_mix"])

    st4 = st.reshape(PEER_HEADS_, 2, PEER_NKEYS_, t)
    rank2, e2, count, c1 = _route(st4, tiles["tr"], tiles["route_group"])

    u_bf = peer_u[0].astype(BF16)
    vt_bf = peer_v[0].T.astype(BF16)
    return _peer(ht, u_bf, vt_bf, rank2, e2, count, c1, x1, mod3, tiles["tt"], tiles["te"])
```

```python
import functools
import math

import jax
import jax.numpy as jnp
from jax import lax
from jax.experimental import pallas as pl
from jax.experimental.pallas import tpu as pltpu

F32 = jnp.float32
BF16 = jnp.bfloat16

D_MODEL_ = 1024
N_MOD_ = 6
NORM_EPS_ = 1e-6
RET_HEADS_ = 4
RET_DK_ = 64
RET_DV_ = 128
RET_CHUNK_ = 128
DIFF_HEADS_ = 4
DIFF_D_ = 64
DIFF_DV_ = 128
GRID_W_ = 64
ROPE_BASE_ = 10000.0
PEER_HEADS_ = 8
PEER_NKEYS_ = 128
PEER_HALF_ = 128
PEER_TOPK_ = 16
LAM_INIT_ = 0.8 - 0.6 * math.exp(-0.3 * 0)

LANES_ = 128
SUBLANES_ = 8
PACKED_LANES_ = 2 * LANES_
VMEM_LIMIT_BYTES_ = 56 * 1024 * 1024

DIFF_Q_SCALE_ = math.log2(math.e) * DIFF_D_ ** -0.5

RET_QK_ = RET_HEADS_ * RET_DK_
RET_W_ = RET_HEADS_ * RET_DV_
DIFF_W_ = DIFF_HEADS_ * DIFF_DV_
IN_COLS_ = 2 * RET_QK_ + 2 * RET_W_ + 3 * DIFF_W_


def _dot(a, b):
    return jnp.dot(a, b, preferred_element_type=F32)


def _dot_nt(a, b):
    return lax.dot_general(a, b, (((1,), (1,)), ((), ())), preferred_element_type=F32)


def _sigmoid(x):
    return 1.0 / (1.0 + jnp.exp(-x))


def _params(sem, flags=None):
    return pltpu.CompilerParams(dimension_semantics=sem, vmem_limit_bytes=VMEM_LIMIT_BYTES_, flags=flags)


def _adaln_kernel(cond_ref, w_ref, b_ref, o_ref):
    a = cond_ref[...]
    a = a * _sigmoid(a)
    o_ref[...] = _dot(a.astype(BF16), w_ref[...].astype(BF16)) + b_ref[...]


def _adaln(cond, w_mod, b_mod):
    rows, d = cond.shape
    n = w_mod.shape[1]
    tn = d
    return pl.pallas_call(
        _adaln_kernel,
        grid=(n // tn,),
        in_specs=[
            pl.BlockSpec((rows, d), lambda j: (0, 0)),
            pl.BlockSpec((d, tn), lambda j: (0, j)),
            pl.BlockSpec((1, tn), lambda j: (0, j)),
        ],
        out_specs=pl.BlockSpec((rows, tn), lambda j: (0, j)),
        out_shape=jax.ShapeDtypeStruct((rows, n), F32),
        compiler_params=_params(("parallel",)),
        name="adaln",
    )(cond, w_mod, b_mod.reshape(1, n))


def _rope_slab(v, cos, sin_up, sin_dn):
    up = pltpu.roll(v, LANES_ - 16, 1)
    dn = pltpu.roll(v, 16, 1)
    return v * cos + up * sin_up + dn * sin_dn


def _group_rms_scale(v, eps):
    w = v.shape[-1]
    sel = (lax.broadcasted_iota(jnp.int32, (w, LANES_), 0) // DIFF_D_
           == lax.broadcasted_iota(jnp.int32, (w, LANES_), 1)).astype(BF16)
    expand = (lax.broadcasted_iota(jnp.int32, (LANES_, w), 1) // DIFF_D_
              == lax.broadcasted_iota(jnp.int32, (LANES_, w), 0)).astype(BF16)
    v2 = v * v
    hi = v2.astype(BF16)
    lo = (v2 - hi.astype(F32)).astype(BF16)
    ssum = _dot(hi, sel) + _dot(lo, sel)
    r = lax.rsqrt(ssum * (1.0 / DIFF_D_) + eps)
    rhi = r.astype(BF16)
    rlo = (r - rhi.astype(F32)).astype(BF16)
    return _dot(rhi, expand) + _dot(rlo, expand)


def _inproj_kernel(x_ref, sh_ref, sc_ref, g_ref, w_ref, cos_ref, su_ref, sd_ref, qkg_ref,
                   rq_ref, rk_ref, rv_ref, rg_ref, dq_ref, dk_ref, dv_ref, y_scr, *, rope):
    x = x_ref[0]
    ms = jnp.mean(x * x, axis=-1, keepdims=True)
    h = x * lax.rsqrt(ms + NORM_EPS_) * g_ref[...]
    h = h * (1.0 + sc_ref[0]) + sh_ref[0]
    y_scr[...] = _dot(h.astype(BF16), w_ref[...])

    cos = cos_ref[...]
    su = su_ref[...]
    sd = sd_ref[...]

    def put(dst_ref, src_col, width, scale, norm_g):
        v = y_scr[:, src_col:src_col + width]
        if norm_g is not None:
            v = v * _group_rms_scale(v, NORM_EPS_) * norm_g
        for s in range(width // LANES_):
            slab = v[:, s * LANES_:(s + 1) * LANES_]
            if rope:
                slab = _rope_slab(slab, cos, su, sd)
            if scale != 1.0:
                slab = slab * scale
            dst_ref[0, :, s * LANES_:(s + 1) * LANES_] = slab.astype(dst_ref.dtype)

    c = 0
    put(rq_ref, c, RET_QK_, 1.0, None)
    c += RET_QK_
    put(rk_ref, c, RET_QK_, RET_DK_ ** -0.5, None)
    c += RET_QK_
    rv_ref[0] = y_scr[:, c:c + RET_W_].astype(BF16)
    c += RET_W_
    rg_ref[0] = y_scr[:, c:c + RET_W_]
    c += RET_W_
    put(dq_ref, c, DIFF_W_, DIFF_Q_SCALE_, qkg_ref[0:1, :])
    c += DIFF_W_
    put(dk_ref, c, DIFF_W_, 1.0, qkg_ref[1:2, :])
    c += DIFF_W_
    dv_ref[0] = y_scr[:, c:c + DIFF_W_].astype(BF16)


def _inproj(xs, mod3, mod_row, norm_g, w_in_bf, tabs, qkg, rope, tm):
    b, l, d = xs.shape
    cos, su, sd = tabs
    grid = (b, l // tm)
    tok = lambda w: pl.BlockSpec((1, tm, w), lambda i, j: (i, j, 0))
    tab = pl.BlockSpec((tm, LANES_), lambda i, j: (j, 0))
    full = lambda a: pl.BlockSpec(a.shape, lambda i, j: (0,) * a.ndim)
    out_shapes = (
        jax.ShapeDtypeStruct((b, l, RET_QK_), BF16),
        jax.ShapeDtypeStruct((b, l, RET_QK_), BF16),
        jax.ShapeDtypeStruct((b, l, RET_W_), BF16),
        jax.ShapeDtypeStruct((b, l, RET_W_), F32),
        jax.ShapeDtypeStruct((b, l, DIFF_W_), BF16),
        jax.ShapeDtypeStruct((b, l, DIFF_W_), BF16),
        jax.ShapeDtypeStruct((b, l, DIFF_W_), BF16),
    )
    return pl.pallas_call(
        functools.partial(_inproj_kernel, rope=rope),
        grid=grid,
        in_specs=[
            tok(d),
            pl.BlockSpec((1, 1, d), lambda i, j: (mod_row(i), 0, 0)),
            pl.BlockSpec((1, 1, d), lambda i, j: (mod_row(i), 0, 1)),
            full(norm_g),
            full(w_in_bf),
            tab, tab, tab,
            full(qkg),
        ],
        out_specs=(tok(RET_QK_), tok(RET_QK_), tok(RET_W_), tok(RET_W_),
                   tok(DIFF_W_), tok(DIFF_W_), tok(DIFF_W_)),
        out_shape=out_shapes,
        scratch_shapes=[pltpu.VMEM((tm, IN_COLS_), F32)],
        compiler_params=_params(("parallel", "parallel")),
        name="inproj_rope" if rope else "inproj_ctx",
    )(xs, mod3, mod3, norm_g, w_in_bf, cos, su, sd, qkg)


def _log_sigmoid(x):
    return jnp.minimum(x, 0.0) - jnp.log(1.0 + jnp.exp(-jnp.abs(x)))


def _retention_kernel(dl_ref, q_ref, k_ref, v_ref, rg_ref, ck_ref, cv_ref, g_ref, o_ref,
                      lf_scr, lb_scr, rf_scr, rb_scr, *, seq, ctx_len):
    c = RET_CHUNK_
    n_chunks = seq // c
    head = pl.program_id(1)
    lane = lax.broadcasted_iota(jnp.int32, (1, LANES_), 1)
    head_lanes = (lane // RET_DK_) == (head % 2)

    dl = dl_ref[0]
    lg_f = _log_sigmoid(dl[0:1, :])
    lg_b = _log_sigmoid(dl[1:2, :])

    row = lax.broadcasted_iota(jnp.int32, (c, LANES_), 0).astype(F32)
    col = lax.broadcasted_iota(jnp.int32, (c, LANES_), 1).astype(F32)
    kw_f = jnp.exp((c - 1.0 - row) * lg_f)
    kw_b = jnp.exp(row * lg_b)
    qw_f = jnp.exp((row + 1.0) * lg_f)
    qw_b = jnp.exp((c - row) * lg_b)
    dist = row - col
    decay = jnp.where(dist >= 0.0,
                      jnp.exp(jnp.maximum(dist, 0.0) * lg_f),
                      jnp.exp(jnp.maximum(-dist, 0.0) * lg_b))
    chunk_f = jnp.exp(float(c) * lg_f)
    chunk_b = jnp.exp(float(c) * lg_b)

    crow = lax.broadcasted_iota(jnp.int32, (ctx_len, LANES_), 0).astype(F32)
    ck = ck_ref[0].astype(F32)
    cv = cv_ref[0]
    s_f = _dot((ck * jnp.exp((ctx_len - 1.0 - crow) * lg_f)).T.astype(BF16), cv)
    s_b = _dot((ck * jnp.exp(crow * lg_b)).T.astype(BF16), cv)

    def local_states(n, carry):
        sl = pl.ds(pl.multiple_of(n * c, c), c)
        kn = k_ref[0, sl, :].astype(F32)
        vn = v_ref[0, sl, :]
        lf_scr[n] = _dot((kn * kw_f).T.astype(BF16), vn)
        lb_scr[n] = _dot((kn * kw_b).T.astype(BF16), vn)
        return carry

    lax.fori_loop(0, n_chunks, local_states, 0, unroll=True)

    def scan_f(n, r):
        rf_scr[n] = r
        return chunk_f * r + lf_scr[n]

    lax.fori_loop(0, n_chunks, scan_f, s_f)

    def scan_b(m, r):
        n = n_chunks - 1 - m
        rb_scr[n] = r
        return chunk_b * r + lb_scr[n]

    lax.fori_loop(0, n_chunks, scan_b, s_b)

    gain = g_ref[0]

    def outputs(n, carry):
        sl = pl.ds(pl.multiple_of(n * c, c), c)
        qn = jnp.where(head_lanes, q_ref[0, sl, :], jnp.zeros((), BF16))
        scores = _dot_nt(qn, k_ref[0, sl, :])
        y = _dot((scores * decay).astype(BF16), v_ref[0, sl, :])
        y = y + qw_f * _dot(qn, rf_scr[n].astype(BF16)) + qw_b * _dot(qn, rb_scr[n].astype(BF16))
        ms = jnp.mean(y * y, axis=-1, keepdims=True)
        yn = y * lax.rsqrt(ms + NORM_EPS_) * gain
        rg = rg_ref[0, sl, :]
        o_ref[0, sl, :] = (yn * (rg * _sigmoid(rg))).astype(o_ref.dtype)
        return carry

    lax.fori_loop(0, n_chunks, outputs, 0, unroll=True)


def _retention(rq, rk, rv, rg, crk, crv, decay_b, norm_g3):
    b, l, _ = rq.shape
    ctx_len = crk.shape[1]
    n_chunks = l // RET_CHUNK_
    pair = lambda length: pl.BlockSpec((1, length, LANES_), lambda i, h: (i, 0, h // 2))
    own = lambda length: pl.BlockSpec((1, length, LANES_), lambda i, h: (i, 0, h))
    state = pltpu.VMEM((n_chunks, LANES_, RET_DV_), F32)
    return pl.pallas_call(
        functools.partial(_retention_kernel, seq=l, ctx_len=ctx_len),
        grid=(b, RET_HEADS_),
        in_specs=[
            pl.BlockSpec((1, 2, LANES_), lambda i, h: (h, 0, 0)),
            pair(l), pair(l), own(l), own(l), pair(ctx_len), own(ctx_len),
            pl.BlockSpec((1, 1, RET_DV_), lambda i, h: (h, 0, 0)),
        ],
        out_specs=own(l),
        out_shape=jax.ShapeDtypeStruct((b, l, RET_W_), BF16),
        scratch_shapes=[state, state, state, state],
        compiler_params=_params(("parallel", "parallel")),
        name="retention",
    )(decay_b, rq, rk, rv, rg, crk, crv, norm_g3)


def _diffattn_kernel(lam_ref, q_ref, k_ref, v_ref, ck_ref, cv_ref, g_ref, o_ref):
    q = q_ref[0]
    lane = lax.broadcasted_iota(jnp.int32, (1, LANES_), 1)
    first = lane < DIFF_D_
    zero = jnp.zeros((), BF16)
    q1 = jnp.where(first, q, zero)
    q2 = jnp.where(first, zero, q)
    lp = lam_ref[...]
    lam = (jnp.exp(jnp.sum(lp[0:1] * lp[1:2], axis=-1, keepdims=True))
           - jnp.exp(jnp.sum(lp[2:3] * lp[3:4], axis=-1, keepdims=True)) + LAM_INIT_)

    def attend(qh):
        s_c = _dot_nt(qh, ck_ref[0])
        m_c = jnp.max(s_c, axis=-1, keepdims=True)
        p_c = jnp.exp2(s_c - m_c)
        l_c = jnp.sum(p_c, axis=-1, keepdims=True)
        a_c = _dot(p_c.astype(BF16), cv_ref[0])
        s_l = _dot_nt(qh, k_ref[0])
        m = jnp.maximum(m_c, jnp.max(s_l, axis=-1, keepdims=True))
        alpha = jnp.exp2(m_c - m)
        p_l = jnp.exp2(s_l - m)
        denom = alpha * l_c + jnp.sum(p_l, axis=-1, keepdims=True)
        return (alpha * a_c + _dot(p_l.astype(BF16), v_ref[0])) / denom

    y = attend(q1) - lam * attend(q2)
    ms = jnp.mean(y * y, axis=-1, keepdims=True)
    yn = y * lax.rsqrt(ms + NORM_EPS_) * g_ref[0]
    o_ref[0] = (yn * (1.0 - LAM_INIT_)).astype(o_ref.dtype)


def _diffattn(dq, dk, dv, cdk, cdv, lam_params, norm_g3, tq):
    b, l, _ = dq.shape
    ctx_len = cdk.shape[1]
    kv = lambda length: pl.BlockSpec((1, length, LANES_), lambda i, h, j: (i, 0, h))
    qo = pl.BlockSpec((1, tq, LANES_), lambda i, h, j: (i, j, h))
    return pl.pallas_call(
        _diffattn_kernel,
        grid=(b, DIFF_HEADS_, l // tq),
        in_specs=[
            pl.BlockSpec(lam_params.shape, lambda i, h, j: (0, 0)),
            qo, kv(l), kv(l), kv(ctx_len), kv(ctx_len),
            pl.BlockSpec((1, 1, DIFF_DV_), lambda i, h, j: (h, 0, 0)),
        ],
        out_specs=qo,
        out_shape=jax.ShapeDtypeStruct((b, l, DIFF_W_), BF16),
        compiler_params=_params(("parallel", "parallel", "parallel")),
        name="diffattn",
    )(lam_params, dq, dk, dv, cdk, cdv, norm_g3)


def _mixout_kernel(ret_ref, dif_ref, x_ref, gate_ref, sh_ref, sc_ref, g_ref, woa_ref, wob_ref,
                   wqt_ref, keys_ref, x1_ref, ht_ref, st_ref):
    o = _dot(ret_ref[0], woa_ref[...]) + _dot(dif_ref[0], wob_ref[...])
    x1 = x_ref[0] + gate_ref[0] * o
    x1_ref[0] = x1
    ms = jnp.mean(x1 * x1, axis=-1, keepdims=True)
    h2 = x1 * lax.rsqrt(ms + NORM_EPS_) * g_ref[...]
    h2 = h2 * (1.0 + sc_ref[0]) + sh_ref[0]
    ht = h2.T.astype(BF16)
    ht_ref[...] = ht
    qt = _dot(wqt_ref[...], ht).astype(BF16)
    for hp in range(2 * PEER_HEADS_):
        st_ref[hp] = _dot(keys_ref[hp], qt[hp * PEER_HALF_:(hp + 1) * PEER_HALF_, :])


def _mixout(ret, dif, x, mod3, norm_g, wo_a, wo_b, wq_t, keys, tm):
    b, l, d = x.shape
    t = b * l
    per = l // tm
    tok = lambda w: pl.BlockSpec((1, tm, w), lambda i, j: (i, j, 0))
    modrow = lambda k: pl.BlockSpec((1, 1, d), lambda i, j: (i, 0, k))
    full = lambda a: pl.BlockSpec(a.shape, lambda i, j: (0,) * a.ndim)
    n_hp = 2 * PEER_HEADS_
    return pl.pallas_call(
        _mixout_kernel,
        grid=(b, per),
        in_specs=[tok(RET_W_), tok(DIFF_W_), tok(d), modrow(2), modrow(3), modrow(4),
                  full(norm_g), full(wo_a), full(wo_b), full(wq_t), full(keys)],
        out_specs=(
            tok(d),
            pl.BlockSpec((d, tm), lambda i, j: (0, i * per + j)),
            pl.BlockSpec((n_hp, PEER_NKEYS_, tm), lambda i, j: (0, 0, i * per + j)),
        ),
        out_shape=(
            jax.ShapeDtypeStruct((b, l, d), F32),
            jax.ShapeDtypeStruct((d, t), BF16),
            jax.ShapeDtypeStruct((n_hp, PEER_NKEYS_, t), F32),
        ),
        compiler_params=_params(("parallel", "parallel")),
        name="mix_out",
    )(ret, dif, x, mod3, mod3, mod3, norm_g, wo_a, wo_b, wq_t, keys)


def _sort16_pairs():
    def merge(lo, hi, r):
        step = r * 2
        if step < hi - lo:
            yield from merge(lo, hi, step)
            yield from merge(lo + r, hi, step)
            for i in range(lo + r, hi - r, step):
                yield (i, i + r)
        else:
            yield (lo, lo + r)

    def sort(lo, hi):
        if hi - lo >= 1:
            mid = lo + (hi - lo) // 2
            yield from sort(lo, mid)
            yield from sort(mid + 1, hi)
            yield from merge(lo, hi, 1)

    return tuple(sort(0, 15))


SORT16_PAIRS_ = _sort16_pairs()


def _bitonic_merge_desc(c):
    c = list(c)
    d = len(c) // 2
    while d >= 1:
        for k in range(len(c)):
            if k & d == 0:
                hi = jnp.maximum(c[k], c[k + d])
                lo = jnp.minimum(c[k], c[k + d])
                c[k], c[k + d] = hi, lo
        d //= 2
    return c


def _merge_top(a, b):
    n = len(a)
    return _bitonic_merge_desc([jnp.maximum(a[k], b[n - 1 - k]) for k in range(n)])


def _merge_sublanes(a):
    for shift in (4, 2, 1):
        a = _merge_top(a, [pltpu.roll(v, shift, 0) for v in a])
    return a


def _top16_rows(x):
    n = PEER_NKEYS_ // SUBLANES_
    a = [x[v * SUBLANES_:(v + 1) * SUBLANES_, :] for v in range(n)]
    for i, j in SORT16_PAIRS_:
        hi = jnp.maximum(a[i], a[j])
        lo = jnp.minimum(a[i], a[j])
        a[i], a[j] = hi, lo
    return _merge_sublanes(a)


def _route_group(s1, s2):
    w = s1.shape[-1]
    k = PEER_TOPK_
    v1 = _top16_rows(s1)
    v2 = _top16_rows(s2)
    sub = lax.broadcasted_iota(jnp.int32, (SUBLANES_, w), 0)
    inf = jnp.float32(jnp.inf)

    def pack(vals):
        out = vals[SUBLANES_ - 1]
        for a in range(SUBLANES_ - 2, -1, -1):
            out = jnp.where(sub == a, vals[a], out)
        return out

    v1_lo = pack(v1[:SUBLANES_])
    v1_hi = pack(v1[SUBLANES_:])
    cand = [jnp.where(sub < k // (b + 1), v1_lo + v2[b], -inf) for b in range(k)]
    cand_hi = v1_hi + v2[0]
    merged = list(cand)
    merged[k - 1] = jnp.maximum(merged[k - 1], cand_hi)
    best = _merge_sublanes(_bitonic_merge_desc(merged))
    top = best[0]
    cut = best[k - 1]
    z = jnp.exp(best[0] - top)
    for b in range(1, k):
        z = z + jnp.exp(best[b] - top)
    rz = (2.0 ** -0.5) / z

    n_lo = jnp.zeros((SUBLANES_, w), F32)
    for b in range(k):
        n_lo = n_lo + jnp.where(cand[b] >= cut, 1.0, 0.0)
    n_hi = jnp.where(cand_hi >= cut, 1.0, 0.0)
    n_rank = ([jnp.broadcast_to(n_lo[a:a + 1, :], (SUBLANES_, w)) for a in range(SUBLANES_)]
              + [jnp.broadcast_to(n_hi[a:a + 1, :], (SUBLANES_, w)) for a in range(SUBLANES_)])

    rank2, e2, count, c1 = [], [], [], []
    for v in range(PEER_NKEYS_ // SUBLANES_):
        rows = slice(v * SUBLANES_, (v + 1) * SUBLANES_)
        x1 = s1[rows, :]
        x2 = s2[rows, :]
        cnt = jnp.zeros((SUBLANES_, w), F32)
        for a in range(k):
            cnt = jnp.where(x1 == v1[a], n_rank[a], cnt)
        rk = jnp.full((SUBLANES_, w), float(k), F32)
        for b in range(k - 1, -1, -1):
            rk = jnp.where(x2 >= v2[b], float(b), rk)
        count.append(cnt)
        c1.append(jnp.exp(x1 - v1[0]) * rz)
        rank2.append(rk)
        e2.append(jnp.exp(x2 - v2[0]))
    cat = lambda parts: jnp.concatenate(parts, axis=0)
    return cat(rank2), cat(e2), cat(count), cat(c1)


def _route_kernel(s_ref, r2_ref, e2_ref, n_ref, c1_ref, *, group):
    tr = s_ref.shape[-1]
    packed = (PEER_NKEYS_ // (2 * SUBLANES_), 2 * SUBLANES_, group)

    def body(g, carry):
        cols = pl.ds(pl.multiple_of(g * group, group), group)
        rank2, e2, count, c1 = _route_group(s_ref[0, 0, :, cols], s_ref[0, 1, :, cols])
        r2_ref[0, :, :, cols] = rank2.astype(BF16).reshape(packed)
        e2_ref[0, :, :, cols] = e2.astype(BF16).reshape(packed)
        n_ref[0, :, cols] = count
        c1_ref[0, :, cols] = c1
        return carry

    lax.fori_loop(0, tr // group, body, 0)


def _route(st4, tr, group):
    heads, _, nk, t = st4.shape
    pk = (nk // (2 * SUBLANES_), 2 * SUBLANES_)
    out_pk = jax.ShapeDtypeStruct((heads,) + pk + (t,), BF16)
    out_f = jax.ShapeDtypeStruct((heads, nk, t), F32)
    spec_pk = pl.BlockSpec((1,) + pk + (tr,), lambda h, j: (h, 0, 0, j))
    spec_f = pl.BlockSpec((1, nk, tr), lambda h, j: (h, 0, j))
    return pl.pallas_call(
        functools.partial(_route_kernel, group=group),
        grid=(heads, t // tr),
        in_specs=[pl.BlockSpec((1, 2, nk, tr), lambda h, j: (h, 0, 0, j))],
        out_specs=(spec_pk, spec_pk, spec_f, spec_f),
        out_shape=(out_pk, out_pk, out_f, out_f),
        compiler_params=_params(("parallel", "parallel")),
        name="peer_route",
    )(st4)


def _peer_kernel(ht_ref, u_ref, vtp_ref, vt_ref, r2_ref, e2_ref, n_ref, c1_ref, x1_ref, gate_ref, o_ref,
                 acc_ref, g_scr, *, chunk, n_blocks):
    e = pl.program_id(1)
    te = u_ref.shape[0]
    tt = ht_ref.shape[1]
    pk = 2 * SUBLANES_
    rows_per_chunk = chunk // PEER_NKEYS_
    zero = jnp.zeros((), BF16)
    n_chunks = te // chunk
    groups = PEER_NKEYS_ // pk

    def fold(vt_any_ref, pair):
        lo = 2 * pair * chunk
        g_pair = g_scr[2 * pair:2 * pair + 2].reshape(2 * chunk, tt)
        acc_ref[...] += _dot(vt_any_ref[:, lo:lo + 2 * chunk], g_pair)

    def weighted_activations(c, a):
        act = (a * (1.0 + lax.erf(a))).astype(BF16)
        act = act.reshape(rows_per_chunk * groups, pk, tt)
        for r in range(rows_per_chunk):
            i = c * rows_per_chunk + r
            spread = lambda ref, h: jnp.broadcast_to(ref[h, i:i + 1, :], (pk, tt)).astype(BF16)[None]
            cnt_rows = [spread(n_ref, h) for h in range(PEER_HEADS_)]
            c1_rows = [spread(c1_ref, h) for h in range(PEER_HEADS_)]
            for lg in range(tt // PACKED_LANES_):
                cols = slice(lg * PACKED_LANES_, (lg + 1) * PACKED_LANES_)
                cnts = [v[:, :, cols] for v in cnt_rows]
                c1s = [v[:, :, cols] for v in c1_rows]
                for s in range(groups):
                    w = None
                    for h in range(PEER_HEADS_):
                        term = jnp.where(r2_ref[h, s:s + 1, :, cols] < cnts[h],
                                         e2_ref[h, s:s + 1, :, cols], zero) * c1s[h]
                        w = term if w is None else w + term
                    g_scr[c, r * groups + s:r * groups + s + 1, :, cols] = (
                        w * act[r * groups + s:r * groups + s + 1, :, cols])

    def step(first):
        ht = ht_ref[...]
        first_dot = lambda c: _dot(u_ref[c * chunk:(c + 1) * chunk, :], ht)
        a_next = first_dot(0)
        for c in range(n_chunks):
            a = a_next
            if c + 1 < n_chunks:
                a_next = first_dot(c + 1)
            if c == 0 and not first:
                fold(vtp_ref, n_chunks // 2 - 1)
            elif c > 0 and c % 2 == 0:
                fold(vt_ref, c // 2 - 1)
            weighted_activations(c, a)

    @pl.when(e == 0)
    def _():
        acc_ref[...] = jnp.zeros_like(acc_ref)
        step(True)

    @pl.when((e > 0) & (e < n_blocks))
    def _():
        step(False)

    @pl.when(e == n_blocks)
    def _():
        fold(vtp_ref, n_chunks // 2 - 1)
        o_ref[0] = x1_ref[0] + gate_ref[0] * acc_ref[...].T


def _peer(ht, u_bf, vt_bf, rank2, e2, count, c1, x1, mod3, tt, te):
    d, t = ht.shape
    n_exp = u_bf.shape[0]
    b, l, _ = x1.shape
    per = l // tt
    rows = te // PEER_NKEYS_
    pk = (PEER_NKEYS_ // (2 * SUBLANES_), 2 * SUBLANES_)
    chunk = 2 * PEER_NKEYS_
    n_blocks = n_exp // te
    g_scratch = pltpu.VMEM((te // chunk, chunk // pk[1], pk[1], tt), BF16)
    cur = lambda e: jnp.minimum(e, n_blocks - 1)
    return pl.pallas_call(
        functools.partial(_peer_kernel, chunk=chunk, n_blocks=n_blocks),
        grid=(t // tt, n_blocks + 1),
        in_specs=[
            pl.BlockSpec((d, tt), lambda i, e: (0, i)),
            pl.BlockSpec((te, d), lambda i, e: (cur(e), 0)),
            pl.BlockSpec((d, te), lambda i, e: (0, jnp.maximum(e - 1, 0))),
            pl.BlockSpec((d, te), lambda i, e: (0, cur(e))),
            pl.BlockSpec((PEER_HEADS_,) + pk + (tt,), lambda i, e: (0, 0, 0, i)),
            pl.BlockSpec((PEER_HEADS_,) + pk + (tt,), lambda i, e: (0, 0, 0, i)),
            pl.BlockSpec((PEER_HEADS_, rows, tt), lambda i, e: (0, cur(e), i)),
            pl.BlockSpec((PEER_HEADS_, rows, tt), lambda i, e: (0, cur(e), i)),
            pl.BlockSpec((1, tt, d), lambda i, e: (i // per, i % per, 0)),
            pl.BlockSpec((1, 1, d), lambda i, e: (i // per, 0, 5)),
        ],
        out_specs=pl.BlockSpec((1, tt, d), lambda i, e: (i // per, i % per, 0)),
        out_shape=jax.ShapeDtypeStruct((b, l, d), F32),
        scratch_shapes=[pltpu.VMEM((d, tt), F32), g_scratch],
        compiler_params=_params(("parallel", "arbitrary")),
        name="peer_dense",
    )(ht, u_bf, vt_bf, vt_bf, rank2, e2, count, c1, x1, mod3)


def _rope_tables(seq):
    quarter = RET_DK_ // 4
    freqs = ROPE_BASE_ ** (-jnp.arange(quarter, dtype=F32) / quarter)
    rows = seq // GRID_W_
    row = jnp.repeat(jnp.arange(rows, dtype=F32), GRID_W_)
    col = jnp.tile(jnp.arange(GRID_W_, dtype=F32), rows)
    ar = row[:, None] * freqs
    ac = col[:, None] * freqs
    ang = jnp.concatenate([ar, ar, ac, ac], axis=-1)
    ang = jnp.concatenate([ang, ang], axis=-1)
    cos, sin = jnp.cos(ang), jnp.sin(ang)
    first = (jnp.arange(LANES_) % 32) < 16
    return cos, jnp.where(first, -sin, 0.0), jnp.where(first, 0.0, sin)


def _tiles(seq, ctx_len, tokens):
    pick = lambda n, pref: next(c for c in pref if n % c == 0)
    return dict(
        tm_in=pick(seq, (512, 256, 128)),
        tm_ctx=pick(ctx_len, (256, 128)),
        tq=pick(seq, (1024, 512, 256, 128)),
        tm_mix=pick(seq, (512, 256, 128)),
        tr=pick(tokens, (1024, 512, 256, 128)),
        route_group=pick(tokens, (256, 128)),
        tt=pick(seq, (1024, 512, 256)),
        te=1024,
    )


def kernel(x, c, ctx, c_ctx, w_mod, b_mod, norm1_g, norm2_g, w_in, ret_decay_logit, ret_norm_g,
           diff_qk_norm_g, diff_lambda, diff_norm_g, w_out, peer_w_query, peer_sub_keys, peer_u, peer_v):
    assert w_mod.shape[0] == 1, "single-layer stack"
    b, l, d = x.shape
    ctx_len = ctx.shape[1]
    t = b * l
    tiles = _tiles(l, ctx_len, t)

    n_rows = -(-(b + 1) // SUBLANES_) * SUBLANES_
    cond = jnp.zeros((n_rows, d), F32).at[:b].set(c).at[b].set(c_ctx)
    mod = _adaln(cond, w_mod[0], b_mod[0])
    mod3 = mod.reshape(n_rows, 1, N_MOD_ * d)

    w_in_bf = w_in[0].astype(BF16)
    g1 = norm1_g[0].reshape(1, d)
    g2 = norm2_g[0].reshape(1, d)
    qkg = jnp.tile(diff_qk_norm_g[0], (1, DIFF_W_ // DIFF_D_))
    tabs = _rope_tables(l)
    rq, rk, rv, rg, dq, dk, dv = _inproj(x, mod3, lambda i: i, g1, w_in_bf, tabs, qkg, True,
                                         tiles["tm_in"])
    ctabs = tuple(jnp.zeros((ctx_len, LANES_), F32) for _ in range(3))
    _, crk, crv, _, _, cdk, cdv = _inproj(ctx, mod3, lambda i: b, g1, w_in_bf, ctabs, qkg, False,
                                          tiles["tm_ctx"])

    decay_b = jnp.broadcast_to(ret_decay_logit[0].T[:, :, None], (RET_HEADS_, 2, LANES_))
    ret = _retention(rq, rk, rv, rg, crk, crv, decay_b, ret_norm_g[0].reshape(RET_HEADS_, 1, RET_DV_))
    dif = _diffattn(dq, dk, dv, cdk, cdv, diff_lambda[0],
                    diff_norm_g[0].reshape(DIFF_HEADS_, 1, DIFF_DV_), tiles["tq"])

    wo = w_out[0].astype(BF16)
    wq_t = peer_w_query[0].T.astype(BF16)
    keys = peer_sub_keys[0].reshape(2 * PEER_HEADS_, PEER_NKEYS_, PEER_HALF_).astype(BF16)
    x1, ht, st = _mixout(ret, dif, x, mod3, g2, wo[:RET_W_], wo[RET_W_:], wq_t, keys, tiles["tm_mix"])

    st4 = st.reshape(PEER_HEADS_, 2, PEER_NKEYS_, t)
    rank2, e2, count, c1 = _route(st4, tiles["tr"], tiles["route_group"])

    u_bf = (peer_u[0] * (2.0 ** -0.5)).astype(BF16)
    vt_bf = peer_v[0].T.astype(BF16)
    return _peer(ht, u_bf, vt_bf, rank2, e2, count, c1, x1, mod3, tiles["tt"], tiles["te"])
```

```python
import functools
import math

import jax
import jax.numpy as jnp
from jax import lax
from jax.experimental import pallas as pl
from jax.experimental.pallas import tpu as pltpu

F32 = jnp.float32
BF16 = jnp.bfloat16

D_MODEL_ = 1024
N_MOD_ = 6
NORM_EPS_ = 1e-6
RET_HEADS_ = 4
RET_DK_ = 64
RET_DV_ = 128
RET_CHUNK_ = 128
DIFF_HEADS_ = 4
DIFF_D_ = 64
DIFF_DV_ = 128
GRID_W_ = 64
ROPE_BASE_ = 10000.0
PEER_HEADS_ = 8
PEER_NKEYS_ = 128
PEER_HALF_ = 128
PEER_TOPK_ = 16
LAM_INIT_ = 0.8 - 0.6 * math.exp(-0.3 * 0)

LANES_ = 128
SUBLANES_ = 8
PACKED_LANES_ = 2 * LANES_
VMEM_LIMIT_BYTES_ = 56 * 1024 * 1024

DIFF_Q_SCALE_ = math.log2(math.e) * DIFF_D_ ** -0.5

RET_QK_ = RET_HEADS_ * RET_DK_
RET_W_ = RET_HEADS_ * RET_DV_
DIFF_W_ = DIFF_HEADS_ * DIFF_DV_
IN_COLS_ = 2 * RET_QK_ + 2 * RET_W_ + 3 * DIFF_W_


def _dot(a, b):
    return jnp.dot(a, b, preferred_element_type=F32)


def _dot_nt(a, b):
    return lax.dot_general(a, b, (((1,), (1,)), ((), ())), preferred_element_type=F32)


def _sigmoid(x):
    return 1.0 / (1.0 + jnp.exp(-x))


def _params(sem, flags=None):
    return pltpu.CompilerParams(dimension_semantics=sem, vmem_limit_bytes=VMEM_LIMIT_BYTES_, flags=flags)


def _adaln_kernel(cond_ref, w_ref, b_ref, o_ref):
    a = cond_ref[...]
    a = a * _sigmoid(a)
    o_ref[...] = _dot(a.astype(BF16), w_ref[...].astype(BF16)) + b_ref[...]


def _adaln(cond, w_mod, b_mod):
    rows, d = cond.shape
    n = w_mod.shape[1]
    tn = d
    return pl.pallas_call(
        _adaln_kernel,
        grid=(n // tn,),
        in_specs=[
            pl.BlockSpec((rows, d), lambda j: (0, 0)),
            pl.BlockSpec((d, tn), lambda j: (0, j)),
            pl.BlockSpec((1, tn), lambda j: (0, j)),
        ],
        out_specs=pl.BlockSpec((rows, tn), lambda j: (0, j)),
        out_shape=jax.ShapeDtypeStruct((rows, n), F32),
        compiler_params=_params(("parallel",)),
        name="adaln",
    )(cond, w_mod, b_mod.reshape(1, n))


def _rope_slab(v, cos, sin_up, sin_dn):
    up = pltpu.roll(v, LANES_ - 16, 1)
    dn = pltpu.roll(v, 16, 1)
    return v * cos + up * sin_up + dn * sin_dn


def _group_rms_scale(v, eps):
    w = v.shape[-1]
    sel = (lax.broadcasted_iota(jnp.int32, (w, LANES_), 0) // DIFF_D_
           == lax.broadcasted_iota(jnp.int32, (w, LANES_), 1)).astype(BF16)
    expand = (lax.broadcasted_iota(jnp.int32, (LANES_, w), 1) // DIFF_D_
              == lax.broadcasted_iota(jnp.int32, (LANES_, w), 0)).astype(BF16)
    v2 = v * v
    hi = v2.astype(BF16)
    lo = (v2 - hi.astype(F32)).astype(BF16)
    ssum = _dot(hi, sel) + _dot(lo, sel)
    r = lax.rsqrt(ssum * (1.0 / DIFF_D_) + eps)
    rhi = r.astype(BF16)
    rlo = (r - rhi.astype(F32)).astype(BF16)
    return _dot(rhi, expand) + _dot(rlo, expand)


def _inproj_kernel(x_ref, sh_ref, sc_ref, g_ref, w_ref, cos_ref, su_ref, sd_ref, qkg_ref,
                   rq_ref, rk_ref, rv_ref, rg_ref, dq_ref, dk_ref, dv_ref, y_scr, *, rope):
    x = x_ref[0]
    ms = jnp.mean(x * x, axis=-1, keepdims=True)
    h = x * lax.rsqrt(ms + NORM_EPS_) * g_ref[...]
    h = h * (1.0 + sc_ref[0]) + sh_ref[0]
    y_scr[...] = _dot(h.astype(BF16), w_ref[...])

    cos = cos_ref[...]
    su = su_ref[...]
    sd = sd_ref[...]

    def put(dst_ref, src_col, width, scale, norm_g):
        v = y_scr[:, src_col:src_col + width]
        if norm_g is not None:
            v = v * _group_rms_scale(v, NORM_EPS_) * norm_g
        for s in range(width // LANES_):
            slab = v[:, s * LANES_:(s + 1) * LANES_]
            if rope:
                slab = _rope_slab(slab, cos, su, sd)
            if scale != 1.0:
                slab = slab * scale
            dst_ref[0, :, s * LANES_:(s + 1) * LANES_] = slab.astype(dst_ref.dtype)

    c = 0
    put(rq_ref, c, RET_QK_, 1.0, None)
    c += RET_QK_
    put(rk_ref, c, RET_QK_, RET_DK_ ** -0.5, None)
    c += RET_QK_
    rv_ref[0] = y_scr[:, c:c + RET_W_].astype(BF16)
    c += RET_W_
    rg_ref[0] = y_scr[:, c:c + RET_W_]
    c += RET_W_
    put(dq_ref, c, DIFF_W_, DIFF_Q_SCALE_, qkg_ref[0:1, :])
    c += DIFF_W_
    put(dk_ref, c, DIFF_W_, 1.0, qkg_ref[1:2, :])
    c += DIFF_W_
    dv_ref[0] = y_scr[:, c:c + DIFF_W_].astype(BF16)


def _inproj(xs, mod3, mod_row, norm_g, w_in_bf, tabs, qkg, rope, tm):
    b, l, d = xs.shape
    cos, su, sd = tabs
    grid = (b, l // tm)
    tok = lambda w: pl.BlockSpec((1, tm, w), lambda i, j: (i, j, 0))
    tab = pl.BlockSpec((tm, LANES_), lambda i, j: (j, 0))
    full = lambda a: pl.BlockSpec(a.shape, lambda i, j: (0,) * a.ndim)
    out_shapes = (
        jax.ShapeDtypeStruct((b, l, RET_QK_), BF16),
        jax.ShapeDtypeStruct((b, l, RET_QK_), BF16),
        jax.ShapeDtypeStruct((b, l, RET_W_), BF16),
        jax.ShapeDtypeStruct((b, l, RET_W_), F32),
        jax.ShapeDtypeStruct((b, l, DIFF_W_), BF16),
        jax.ShapeDtypeStruct((b, l, DIFF_W_), BF16),
        jax.ShapeDtypeStruct((b, l, DIFF_W_), BF16),
    )
    return pl.pallas_call(
        functools.partial(_inproj_kernel, rope=rope),
        grid=grid,
        in_specs=[
            tok(d),
            pl.BlockSpec((1, 1, d), lambda i, j: (mod_row(i), 0, 0)),
            pl.BlockSpec((1, 1, d), lambda i, j: (mod_row(i), 0, 1)),
            full(norm_g),
            full(w_in_bf),
            tab, tab, tab,
            full(qkg),
        ],
        out_specs=(tok(RET_QK_), tok(RET_QK_), tok(RET_W_), tok(RET_W_),
                   tok(DIFF_W_), tok(DIFF_W_), tok(DIFF_W_)),
        out_shape=out_shapes,
        scratch_shapes=[pltpu.VMEM((tm, IN_COLS_), F32)],
        compiler_params=_params(("parallel", "parallel")),
        name="inproj_rope" if rope else "inproj_ctx",
    )(xs, mod3, mod3, norm_g, w_in_bf, cos, su, sd, qkg)


def _log_sigmoid(x):
    return jnp.minimum(x, 0.0) - jnp.log(1.0 + jnp.exp(-jnp.abs(x)))


def _retention_kernel(dl_ref, q_ref, k_ref, v_ref, rg_ref, ck_ref, cv_ref, g_ref, o_ref,
                      lf_scr, lb_scr, rf_scr, rb_scr, *, seq, ctx_len):
    c = RET_CHUNK_
    n_chunks = seq // c
    head = pl.program_id(1)
    lane = lax.broadcasted_iota(jnp.int32, (1, LANES_), 1)
    head_lanes = (lane // RET_DK_) == (head % 2)

    dl = dl_ref[0]
    lg_f = _log_sigmoid(dl[0:1, :])
    lg_b = _log_sigmoid(dl[1:2, :])

    row = lax.broadcasted_iota(jnp.int32, (c, LANES_), 0).astype(F32)
    col = lax.broadcasted_iota(jnp.int32, (c, LANES_), 1).astype(F32)
    kw_f = jnp.exp((c - 1.0 - row) * lg_f)
    kw_b = jnp.exp(row * lg_b)
    qw_f = jnp.exp((row + 1.0) * lg_f)
    qw_b = jnp.exp((c - row) * lg_b)
    dist = row - col
    decay = jnp.where(dist >= 0.0,
                      jnp.exp(jnp.maximum(dist, 0.0) * lg_f),
                      jnp.exp(jnp.maximum(-dist, 0.0) * lg_b))
    chunk_f = jnp.exp(float(c) * lg_f)
    chunk_b = jnp.exp(float(c) * lg_b)

    crow = lax.broadcasted_iota(jnp.int32, (ctx_len, LANES_), 0).astype(F32)
    ck = ck_ref[0].astype(F32)
    cv = cv_ref[0]
    s_f = _dot((ck * jnp.exp((ctx_len - 1.0 - crow) * lg_f)).T.astype(BF16), cv)
    s_b = _dot((ck * jnp.exp(crow * lg_b)).T.astype(BF16), cv)

    def local_states(n, carry):
        sl = pl.ds(pl.multiple_of(n * c, c), c)
        kn = k_ref[0, sl, :].astype(F32)
        vn = v_ref[0, sl, :]
        lf_scr[n] = _dot((kn * kw_f).T.astype(BF16), vn)
        lb_scr[n] = _dot((kn * kw_b).T.astype(BF16), vn)
        return carry

    lax.fori_loop(0, n_chunks, local_states, 0, unroll=True)

    def scan_f(n, r):
        rf_scr[n] = r
        return chunk_f * r + lf_scr[n]

    lax.fori_loop(0, n_chunks, scan_f, s_f)

    def scan_b(m, r):
        n = n_chunks - 1 - m
        rb_scr[n] = r
        return chunk_b * r + lb_scr[n]

    lax.fori_loop(0, n_chunks, scan_b, s_b)

    gain = g_ref[0]

    def outputs(n, carry):
        sl = pl.ds(pl.multiple_of(n * c, c), c)
        qn = jnp.where(head_lanes, q_ref[0, sl, :], jnp.zeros((), BF16))
        scores = _dot_nt(qn, k_ref[0, sl, :])
        y = _dot((scores * decay).astype(BF16), v_ref[0, sl, :])
        y = y + qw_f * _dot(qn, rf_scr[n].astype(BF16)) + qw_b * _dot(qn, rb_scr[n].astype(BF16))
        ms = jnp.mean(y * y, axis=-1, keepdims=True)
        yn = y * lax.rsqrt(ms + NORM_EPS_) * gain
        rg = rg_ref[0, sl, :]
        o_ref[0, sl, :] = (yn * (rg * _sigmoid(rg))).astype(o_ref.dtype)
        return carry

    lax.fori_loop(0, n_chunks, outputs, 0, unroll=True)


def _retention(rq, rk, rv, rg, crk, crv, decay_b, norm_g3):
    b, l, _ = rq.shape
    ctx_len = crk.shape[1]
    n_chunks = l // RET_CHUNK_
    pair = lambda length: pl.BlockSpec((1, length, LANES_), lambda i, h: (i, 0, h // 2))
    own = lambda length: pl.BlockSpec((1, length, LANES_), lambda i, h: (i, 0, h))
    state = pltpu.VMEM((n_chunks, LANES_, RET_DV_), F32)
    return pl.pallas_call(
        functools.partial(_retention_kernel, seq=l, ctx_len=ctx_len),
        grid=(b, RET_HEADS_),
        in_specs=[
            pl.BlockSpec((1, 2, LANES_), lambda i, h: (h, 0, 0)),
            pair(l), pair(l), own(l), own(l), pair(ctx_len), own(ctx_len),
            pl.BlockSpec((1, 1, RET_DV_), lambda i, h: (h, 0, 0)),
        ],
        out_specs=own(l),
        out_shape=jax.ShapeDtypeStruct((b, l, RET_W_), BF16),
        scratch_shapes=[state, state, state, state],
        compiler_params=_params(("parallel", "parallel")),
        name="retention",
    )(decay_b, rq, rk, rv, rg, crk, crv, norm_g3)


def _diffattn_kernel(lam_ref, q_ref, k_ref, v_ref, ck_ref, cv_ref, g_ref, o_ref):
    q = q_ref[0]
    lane = lax.broadcasted_iota(jnp.int32, (1, LANES_), 1)
    first = lane < DIFF_D_
    zero = jnp.zeros((), BF16)
    q1 = jnp.where(first, q, zero)
    q2 = jnp.where(first, zero, q)
    lp = lam_ref[...]
    lam = (jnp.exp(jnp.sum(lp[0:1] * lp[1:2], axis=-1, keepdims=True))
           - jnp.exp(jnp.sum(lp[2:3] * lp[3:4], axis=-1, keepdims=True)) + LAM_INIT_)

    def attend(qh):
        s_c = _dot_nt(qh, ck_ref[0])
        m_c = jnp.max(s_c, axis=-1, keepdims=True)
        p_c = jnp.exp2(s_c - m_c)
        l_c = jnp.sum(p_c, axis=-1, keepdims=True)
        a_c = _dot(p_c.astype(BF16), cv_ref[0])
        s_l = _dot_nt(qh, k_ref[0])
        m = jnp.maximum(m_c, jnp.max(s_l, axis=-1, keepdims=True))
        alpha = jnp.exp2(m_c - m)
        p_l = jnp.exp2(s_l - m)
        denom = alpha * l_c + jnp.sum(p_l, axis=-1, keepdims=True)
        return (alpha * a_c + _dot(p_l.astype(BF16), v_ref[0])) / denom

    y = attend(q1) - lam * attend(q2)
    ms = jnp.mean(y * y, axis=-1, keepdims=True)
    yn = y * lax.rsqrt(ms + NORM_EPS_) * g_ref[0]
    o_ref[0] = (yn * (1.0 - LAM_INIT_)).astype(o_ref.dtype)


def _diffattn(dq, dk, dv, cdk, cdv, lam_params, norm_g3, tq):
    b, l, _ = dq.shape
    ctx_len = cdk.shape[1]
    kv = lambda length: pl.BlockSpec((1, length, LANES_), lambda i, h, j: (i, 0, h))
    qo = pl.BlockSpec((1, tq, LANES_), lambda i, h, j: (i, j, h))
    return pl.pallas_call(
        _diffattn_kernel,
        grid=(b, DIFF_HEADS_, l // tq),
        in_specs=[
            pl.BlockSpec(lam_params.shape, lambda i, h, j: (0, 0)),
            qo, kv(l), kv(l), kv(ctx_len), kv(ctx_len),
            pl.BlockSpec((1, 1, DIFF_DV_), lambda i, h, j: (h, 0, 0)),
        ],
        out_specs=qo,
        out_shape=jax.ShapeDtypeStruct((b, l, DIFF_W_), BF16),
        compiler_params=_params(("parallel", "parallel", "parallel")),
        name="diffattn",
    )(lam_params, dq, dk, dv, cdk, cdv, norm_g3)


def _mixout_kernel(ret_ref, dif_ref, x_ref, gate_ref, sh_ref, sc_ref, g_ref, woa_ref, wob_ref,
                   wqt_ref, keys_ref, x1_ref, ht_ref, st_ref):
    o = _dot(ret_ref[0], woa_ref[...]) + _dot(dif_ref[0], wob_ref[...])
    x1 = x_ref[0] + gate_ref[0] * o
    x1_ref[0] = x1
    ms = jnp.mean(x1 * x1, axis=-1, keepdims=True)
    h2 = x1 * lax.rsqrt(ms + NORM_EPS_) * g_ref[...]
    h2 = h2 * (1.0 + sc_ref[0]) + sh_ref[0]
    ht = h2.T.astype(BF16)
    ht_ref[...] = ht
    qt = _dot(wqt_ref[...], ht).astype(BF16)
    for hp in range(2 * PEER_HEADS_):
        st_ref[hp] = _dot(keys_ref[hp], qt[hp * PEER_HALF_:(hp + 1) * PEER_HALF_, :])


def _mixout(ret, dif, x, mod3, norm_g, wo_a, wo_b, wq_t, keys, tm):
    b, l, d = x.shape
    t = b * l
    per = l // tm
    tok = lambda w: pl.BlockSpec((1, tm, w), lambda i, j: (i, j, 0))
    modrow = lambda k: pl.BlockSpec((1, 1, d), lambda i, j: (i, 0, k))
    full = lambda a: pl.BlockSpec(a.shape, lambda i, j: (0,) * a.ndim)
    n_hp = 2 * PEER_HEADS_
    return pl.pallas_call(
        _mixout_kernel,
        grid=(b, per),
        in_specs=[tok(RET_W_), tok(DIFF_W_), tok(d), modrow(2), modrow(3), modrow(4),
                  full(norm_g), full(wo_a), full(wo_b), full(wq_t), full(keys)],
        out_specs=(
            tok(d),
            pl.BlockSpec((d, tm), lambda i, j: (0, i * per + j)),
            pl.BlockSpec((n_hp, PEER_NKEYS_, tm), lambda i, j: (0, 0, i * per + j)),
        ),
        out_shape=(
            jax.ShapeDtypeStruct((b, l, d), F32),
            jax.ShapeDtypeStruct((d, t), BF16),
            jax.ShapeDtypeStruct((n_hp, PEER_NKEYS_, t), F32),
        ),
        compiler_params=_params(("parallel", "parallel")),
        name="mix_out",
    )(ret, dif, x, mod3, mod3, mod3, norm_g, wo_a, wo_b, wq_t, keys)


def _sort16_pairs():
    def merge(lo, hi, r):
        step = r * 2
        if step < hi - lo:
            yield from merge(lo, hi, step)
            yield from merge(lo + r, hi, step)
            for i in range(lo + r, hi - r, step):
                yield (i, i + r)
        else:
            yield (lo, lo + r)

    def sort(lo, hi):
        if hi - lo >= 1:
            mid = lo + (hi - lo) // 2
            yield from sort(lo, mid)
            yield from sort(mid + 1, hi)
            yield from merge(lo, hi, 1)

    return tuple(sort(0, 15))


SORT16_PAIRS_ = _sort16_pairs()


def _bitonic_merge_desc(c):
    c = list(c)
    d = len(c) // 2
    while d >= 1:
        for k in range(len(c)):
            if k & d == 0:
                hi = jnp.maximum(c[k], c[k + d])
                lo = jnp.minimum(c[k], c[k + d])
                c[k], c[k + d] = hi, lo
        d //= 2
    return c


def _merge_top(a, b):
    n = len(a)
    return _bitonic_merge_desc([jnp.maximum(a[k], b[n - 1 - k]) for k in range(n)])


def _merge_sublanes(a):
    for shift in (4, 2, 1):
        a = _merge_top(a, [pltpu.roll(v, shift, 0) for v in a])
    return a


def _top16_rows(x):
    n = PEER_NKEYS_ // SUBLANES_
    a = [x[v * SUBLANES_:(v + 1) * SUBLANES_, :] for v in range(n)]
    for i, j in SORT16_PAIRS_:
        hi = jnp.maximum(a[i], a[j])
        lo = jnp.minimum(a[i], a[j])
        a[i], a[j] = hi, lo
    return _merge_sublanes(a)


def _route_group(s1, s2):
    w = s1.shape[-1]
    k = PEER_TOPK_
    v1 = _top16_rows(s1)
    v2 = _top16_rows(s2)
    sub = lax.broadcasted_iota(jnp.int32, (SUBLANES_, w), 0)
    inf = jnp.float32(jnp.inf)

    def pack(vals):
        out = vals[SUBLANES_ - 1]
        for a in range(SUBLANES_ - 2, -1, -1):
            out = jnp.where(sub == a, vals[a], out)
        return out

    v1_lo = pack(v1[:SUBLANES_])
    v1_hi = pack(v1[SUBLANES_:])
    cand = [jnp.where(sub < k // (b + 1), v1_lo + v2[b], -inf) for b in range(k)]
    cand_hi = v1_hi + v2[0]
    merged = list(cand)
    merged[k - 1] = jnp.maximum(merged[k - 1], cand_hi)
    best = _merge_sublanes(_bitonic_merge_desc(merged))
    top = best[0]
    cut = best[k - 1]
    z = jnp.exp(best[0] - top)
    for b in range(1, k):
        z = z + jnp.exp(best[b] - top)
    rz = (2.0 ** -0.5) / z

    n_lo = jnp.zeros((SUBLANES_, w), F32)
    for b in range(k):
        n_lo = n_lo + jnp.where(cand[b] >= cut, 1.0, 0.0)
    n_hi = jnp.where(cand_hi >= cut, 1.0, 0.0)
    n_rank = ([jnp.broadcast_to(n_lo[a:a + 1, :], (SUBLANES_, w)) for a in range(SUBLANES_)]
              + [jnp.broadcast_to(n_hi[a:a + 1, :], (SUBLANES_, w)) for a in range(SUBLANES_)])

    rank2, e2, count, c1 = [], [], [], []
    for v in range(PEER_NKEYS_ // SUBLANES_):
        rows = slice(v * SUBLANES_, (v + 1) * SUBLANES_)
        x1 = s1[rows, :]
        x2 = s2[rows, :]
        cnt = jnp.zeros((SUBLANES_, w), F32)
        for a in range(k):
            cnt = jnp.where(x1 == v1[a], n_rank[a], cnt)
        rk = jnp.full((SUBLANES_, w), float(k), F32)
        for b in range(k - 1, -1, -1):
            rk = jnp.where(x2 >= v2[b], float(b), rk)
        count.append(cnt)
        c1.append(jnp.exp(x1 - v1[0]) * rz)
        rank2.append(rk)
        e2.append(jnp.exp(x2 - v2[0]))
    cat = lambda parts: jnp.concatenate(parts, axis=0)
    return cat(rank2), cat(e2), cat(count), cat(c1)


def _route_kernel(s_ref, r2_ref, e2_ref, n_ref, c1_ref, *, group):
    tr = s_ref.shape[-1]
    packed = (PEER_NKEYS_ // (2 * SUBLANES_), 2 * SUBLANES_, group)

    def body(g, carry):
        cols = pl.ds(pl.multiple_of(g * group, group), group)
        rank2, e2, count, c1 = _route_group(s_ref[0, 0, :, cols], s_ref[0, 1, :, cols])
        r2_ref[0, :, :, cols] = rank2.astype(BF16).reshape(packed)
        e2_ref[0, :, :, cols] = e2.astype(BF16).reshape(packed)
        n_ref[0, :, cols] = count
        c1_ref[0, :, cols] = c1
        return carry

    lax.fori_loop(0, tr // group, body, 0)


def _route(st4, tr, group):
    heads, _, nk, t = st4.shape
    pk = (nk // (2 * SUBLANES_), 2 * SUBLANES_)
    out_pk = jax.ShapeDtypeStruct((heads,) + pk + (t,), BF16)
    out_f = jax.ShapeDtypeStruct((heads, nk, t), F32)
    spec_pk = pl.BlockSpec((1,) + pk + (tr,), lambda h, j: (h, 0, 0, j))
    spec_f = pl.BlockSpec((1, nk, tr), lambda h, j: (h, 0, j))
    return pl.pallas_call(
        functools.partial(_route_kernel, group=group),
        grid=(heads, t // tr),
        in_specs=[pl.BlockSpec((1, 2, nk, tr), lambda h, j: (h, 0, 0, j))],
        out_specs=(spec_pk, spec_pk, spec_f, spec_f),
        out_shape=(out_pk, out_pk, out_f, out_f),
        compiler_params=_params(("parallel", "parallel")),
        name="peer_route",
    )(st4)


def _peer_kernel(ht_ref, u_ref, vt_ref, r2_ref, e2_ref, n_ref, c1_ref, x1_ref, gate_ref, o_ref,
                 acc_ref, g_scr, *, chunk):
    e = pl.program_id(1)
    te = u_ref.shape[0]
    tt = ht_ref.shape[1]
    pk = 2 * SUBLANES_
    rows_per_chunk = chunk // PEER_NKEYS_
    zero = jnp.zeros((), BF16)
    n_chunks = te // chunk
    groups = PEER_NKEYS_ // pk

    def fold(pair):
        lo = 2 * pair * chunk
        g_pair = g_scr[2 * pair:2 * pair + 2].reshape(2 * chunk, tt)
        acc_ref[...] += _dot(vt_ref[:, lo:lo + 2 * chunk], g_pair)

    def weighted_activations(c, a):
        act = (a * (1.0 + lax.erf(a))).astype(BF16)
        act = act.reshape(rows_per_chunk * groups, pk, tt)
        for r in range(rows_per_chunk):
            i = c * rows_per_chunk + r
            spread = lambda ref, h: jnp.broadcast_to(ref[h, i:i + 1, :], (pk, tt)).astype(BF16)[None]
            cnt_rows = [spread(n_ref, h) for h in range(PEER_HEADS_)]
            c1_rows = [spread(c1_ref, h) for h in range(PEER_HEADS_)]
            for lg in range(tt // PACKED_LANES_):
                cols = slice(lg * PACKED_LANES_, (lg + 1) * PACKED_LANES_)
                cnts = [v[:, :, cols] for v in cnt_rows]
                c1s = [v[:, :, cols] for v in c1_rows]
                for s in range(groups):
                    w = None
                    for h in range(PEER_HEADS_):
                        term = jnp.where(r2_ref[h, s:s + 1, :, cols] < cnts[h],
                                         e2_ref[h, s:s + 1, :, cols], zero) * c1s[h]
                        w = term if w is None else w + term
                    g_scr[c, r * groups + s:r * groups + s + 1, :, cols] = (
                        w * act[r * groups + s:r * groups + s + 1, :, cols])

    @pl.when(e == 0)
    def _():
        acc_ref[...] = jnp.zeros_like(acc_ref)

    ht = ht_ref[...]
    first_dot = lambda c: _dot(u_ref[c * chunk:(c + 1) * chunk, :], ht)
    a_next = first_dot(0)
    for c in range(n_chunks):
        a = a_next
        if c + 1 < n_chunks:
            a_next = first_dot(c + 1)
        weighted_activations(c, a)
        if c % 2 == 1:
            fold(c // 2)

    @pl.when(e == pl.num_programs(1) - 1)
    def _():
        o_ref[0] = x1_ref[0] + gate_ref[0] * acc_ref[...].T


def _peer(ht, u_bf, vt_bf, rank2, e2, count, c1, x1, mod3, tt, te):
    d, t = ht.shape
    n_exp = u_bf.shape[0]
    b, l, _ = x1.shape
    per = l // tt
    rows = te // PEER_NKEYS_
    pk = (PEER_NKEYS_ // (2 * SUBLANES_), 2 * SUBLANES_)
    chunk = 2 * PEER_NKEYS_
    g_scratch = pltpu.VMEM((te // chunk, chunk // pk[1], pk[1], tt), BF16)
    return pl.pallas_call(
        functools.partial(_peer_kernel, chunk=chunk),
        grid=(t // tt, n_exp // te),
        in_specs=[
            pl.BlockSpec((d, tt), lambda i, e: (0, i)),
            pl.BlockSpec((te, d), lambda i, e: (e, 0)),
            pl.BlockSpec((d, te), lambda i, e: (0, e)),
            pl.BlockSpec((PEER_HEADS_,) + pk + (tt,), lambda i, e: (0, 0, 0, i)),
            pl.BlockSpec((PEER_HEADS_,) + pk + (tt,), lambda i, e: (0, 0, 0, i)),
            pl.BlockSpec((PEER_HEADS_, rows, tt), lambda i, e: (0, e, i)),
            pl.BlockSpec((PEER_HEADS_, rows, tt), lambda i, e: (0, e, i)),
            pl.BlockSpec((1, tt, d), lambda i, e: (i // per, i % per, 0)),
            pl.BlockSpec((1, 1, d), lambda i, e: (i // per, 0, 5)),
        ],
        out_specs=pl.BlockSpec((1, tt, d), lambda i, e: (i // per, i % per, 0)),
        out_shape=jax.ShapeDtypeStruct((b, l, d), F32),
        scratch_shapes=[pltpu.VMEM((d, tt), F32), g_scratch],
        compiler_params=_params(("parallel", "arbitrary")),
        name="peer_dense",
    )(ht, u_bf, vt_bf, rank2, e2, count, c1, x1, mod3)


def _transpose_cast_kernel(x_ref, o_ref):
    o_ref[...] = x_ref[...].T.astype(o_ref.dtype)


def _transpose_cast(x, tile):
    r, c = x.shape
    return pl.pallas_call(
        _transpose_cast_kernel,
        grid=(r // tile,),
        in_specs=[pl.BlockSpec((tile, c), lambda i: (i, 0))],
        out_specs=pl.BlockSpec((c, tile), lambda i: (0, i)),
        out_shape=jax.ShapeDtypeStruct((c, r), BF16),
        compiler_params=_params(("parallel",)),
        name="transpose_cast",
    )(x)


def _rope_tables(seq):
    quarter = RET_DK_ // 4
    freqs = ROPE_BASE_ ** (-jnp.arange(quarter, dtype=F32) / quarter)
    rows = seq // GRID_W_
    row = jnp.repeat(jnp.arange(rows, dtype=F32), GRID_W_)
    col = jnp.tile(jnp.arange(GRID_W_, dtype=F32), rows)
    ar = row[:, None] * freqs
    ac = col[:, None] * freqs
    ang = jnp.concatenate([ar, ar, ac, ac], axis=-1)
    ang = jnp.concatenate([ang, ang], axis=-1)
    cos, sin = jnp.cos(ang), jnp.sin(ang)
    first = (jnp.arange(LANES_) % 32) < 16
    return cos, jnp.where(first, -sin, 0.0), jnp.where(first, 0.0, sin)


def _tiles(seq, ctx_len, tokens):
    pick = lambda n, pref: next(c for c in pref if n % c == 0)
    return dict(
        tm_in=pick(seq, (512, 256, 128)),
        tm_ctx=pick(ctx_len, (256, 128)),
        tq=pick(seq, (2048, 1024, 512, 256, 128)),
        tm_mix=pick(seq, (512, 256, 128)),
        tr=pick(tokens, (1024, 512, 256, 128)),
        route_group=pick(tokens, (256, 128)),
        tt=pick(seq, (1024, 512, 256)),
        te=1024,
    )


def kernel(x, c, ctx, c_ctx, w_mod, b_mod, norm1_g, norm2_g, w_in, ret_decay_logit, ret_norm_g,
           diff_qk_norm_g, diff_lambda, diff_norm_g, w_out, peer_w_query, peer_sub_keys, peer_u, peer_v):
    assert w_mod.shape[0] == 1, "single-layer stack"
    b, l, d = x.shape
    ctx_len = ctx.shape[1]
    t = b * l
    tiles = _tiles(l, ctx_len, t)

    n_rows = -(-(b + 1) // SUBLANES_) * SUBLANES_
    cond = jnp.zeros((n_rows, d), F32).at[:b].set(c).at[b].set(c_ctx)
    mod = _adaln(cond, w_mod[0], b_mod[0])
    mod3 = mod.reshape(n_rows, 1, N_MOD_ * d)

    w_in_bf = w_in[0].astype(BF16)
    g1 = norm1_g[0].reshape(1, d)
    g2 = norm2_g[0].reshape(1, d)
    qkg = jnp.tile(diff_qk_norm_g[0], (1, DIFF_W_ // DIFF_D_))
    tabs = _rope_tables(l)
    rq, rk, rv, rg, dq, dk, dv = _inproj(x, mod3, lambda i: i, g1, w_in_bf, tabs, qkg, True,
                                         tiles["tm_in"])
    ctabs = tuple(jnp.zeros((ctx_len, LANES_), F32) for _ in range(3))
    _, crk, crv, _, _, cdk, cdv = _inproj(ctx, mod3, lambda i: b, g1, w_in_bf, ctabs, qkg, False,
                                          tiles["tm_ctx"])

    decay_b = jnp.broadcast_to(ret_decay_logit[0].T[:, :, None], (RET_HEADS_, 2, LANES_))
    ret = _retention(rq, rk, rv, rg, crk, crv, decay_b, ret_norm_g[0].reshape(RET_HEADS_, 1, RET_DV_))
    dif = _diffattn(dq, dk, dv, cdk, cdv, diff_lambda[0],
                    diff_norm_g[0].reshape(DIFF_HEADS_, 1, DIFF_DV_), tiles["tq"])

    wo = w_out[0].astype(BF16)
    wq_t = peer_w_query[0].T.astype(BF16)
    keys = peer_sub_keys[0].reshape(2 * PEER_HEADS_, PEER_NKEYS_, PEER_HALF_).astype(BF16)
    x1, ht, st = _mixout(ret, dif, x, mod3, g2, wo[:RET_W_], wo[RET_W_:], wq_t, keys, tiles["tm_mix"])

    st4 = st.reshape(PEER_HEADS_, 2, PEER_NKEYS_, t)
    rank2, e2, count, c1 = _route(st4, tiles["tr"], tiles["route_group"])

    u_bf = (peer_u[0] * (2.0 ** -0.5)).astype(BF16)
    vt_bf = _transpose_cast(peer_v[0], tiles["te"])
    return _peer(ht, u_bf, vt_bf, rank2, e2, count, c1, x1, mod3, tiles["tt"], tiles["te"])
```

```python
import functools
import math

import jax
import jax.numpy as jnp
from jax import lax
from jax.experimental import pallas as pl
from jax.experimental.pallas import tpu as pltpu

F32 = jnp.float32
BF16 = jnp.bfloat16

D_MODEL_ = 1024
N_MOD_ = 6
NORM_EPS_ = 1e-6
RET_HEADS_ = 4
RET_DK_ = 64
RET_DV_ = 128
RET_CHUNK_ = 128
DIFF_HEADS_ = 4
DIFF_D_ = 64
DIFF_DV_ = 128
GRID_W_ = 64
ROPE_BASE_ = 10000.0
PEER_HEADS_ = 8
PEER_NKEYS_ = 128
PEER_HALF_ = 128
PEER_TOPK_ = 16
LAM_INIT_ = 0.8 - 0.6 * math.exp(-0.3 * 0)

LANES_ = 128
SUBLANES_ = 8
PACKED_LANES_ = 2 * LANES_
VMEM_LIMIT_BYTES_ = 60 * 1024 * 1024

DIFF_Q_SCALE_ = math.log2(math.e) * DIFF_D_ ** -0.5

RET_QK_ = RET_HEADS_ * RET_DK_
RET_W_ = RET_HEADS_ * RET_DV_
DIFF_W_ = DIFF_HEADS_ * DIFF_DV_
IN_COLS_ = 2 * RET_QK_ + 2 * RET_W_ + 3 * DIFF_W_


def _dot(a, b):
    return jnp.dot(a, b, preferred_element_type=F32)


def _dot_nt(a, b):
    return lax.dot_general(a, b, (((1,), (1,)), ((), ())), preferred_element_type=F32)


def _sigmoid(x):
    return 1.0 / (1.0 + jnp.exp(-x))


def _params(sem, flags=None):
    return pltpu.CompilerParams(dimension_semantics=sem, vmem_limit_bytes=VMEM_LIMIT_BYTES_, flags=flags)


def _adaln_kernel(cond_ref, w_ref, b_ref, o_ref):
    a = cond_ref[...]
    a = a * _sigmoid(a)
    o_ref[...] = _dot(a.astype(BF16), w_ref[...].astype(BF16)) + b_ref[...]


def _adaln(cond, w_mod, b_mod):
    rows, d = cond.shape
    n = w_mod.shape[1]
    tn = d
    return pl.pallas_call(
        _adaln_kernel,
        grid=(n // tn,),
        in_specs=[
            pl.BlockSpec((rows, d), lambda j: (0, 0)),
            pl.BlockSpec((d, tn), lambda j: (0, j)),
            pl.BlockSpec((1, tn), lambda j: (0, j)),
        ],
        out_specs=pl.BlockSpec((rows, tn), lambda j: (0, j)),
        out_shape=jax.ShapeDtypeStruct((rows, n), F32),
        compiler_params=_params(("parallel",)),
        name="adaln",
    )(cond, w_mod, b_mod.reshape(1, n))


def _rope_slab(v, cos, sin_up, sin_dn):
    up = pltpu.roll(v, LANES_ - 16, 1)
    dn = pltpu.roll(v, 16, 1)
    return v * cos + up * sin_up + dn * sin_dn


def _group_rms_scale(v, eps):
    w = v.shape[-1]
    sel = (lax.broadcasted_iota(jnp.int32, (w, LANES_), 0) // DIFF_D_
           == lax.broadcasted_iota(jnp.int32, (w, LANES_), 1)).astype(BF16)
    expand = (lax.broadcasted_iota(jnp.int32, (LANES_, w), 1) // DIFF_D_
              == lax.broadcasted_iota(jnp.int32, (LANES_, w), 0)).astype(BF16)
    v2 = v * v
    hi = v2.astype(BF16)
    lo = (v2 - hi.astype(F32)).astype(BF16)
    ssum = _dot(hi, sel) + _dot(lo, sel)
    r = lax.rsqrt(ssum * (1.0 / DIFF_D_) + eps)
    rhi = r.astype(BF16)
    rlo = (r - rhi.astype(F32)).astype(BF16)
    return _dot(rhi, expand) + _dot(rlo, expand)


def _inproj_kernel(x_ref, sh_ref, sc_ref, g_ref, w_ref, cos_ref, su_ref, sd_ref, qkg_ref,
                   rq_ref, rk_ref, rv_ref, rg_ref, dq_ref, dk_ref, dv_ref, y_scr, *, rope):
    x = x_ref[0]
    ms = jnp.mean(x * x, axis=-1, keepdims=True)
    h = x * lax.rsqrt(ms + NORM_EPS_) * g_ref[...]
    h = h * (1.0 + sc_ref[0]) + sh_ref[0]
    y_scr[...] = _dot(h.astype(BF16), w_ref[...])

    cos = cos_ref[...]
    su = su_ref[...]
    sd = sd_ref[...]

    def put(dst_ref, src_col, width, scale, norm_g):
        v = y_scr[:, src_col:src_col + width]
        if norm_g is not None:
            v = v * _group_rms_scale(v, NORM_EPS_) * norm_g
        for s in range(width // LANES_):
            slab = v[:, s * LANES_:(s + 1) * LANES_]
            if rope:
                slab = _rope_slab(slab, cos, su, sd)
            if scale != 1.0:
                slab = slab * scale
            dst_ref[0, :, s * LANES_:(s + 1) * LANES_] = slab.astype(dst_ref.dtype)

    c = 0
    put(rq_ref, c, RET_QK_, 1.0, None)
    c += RET_QK_
    put(rk_ref, c, RET_QK_, RET_DK_ ** -0.5, None)
    c += RET_QK_
    rv_ref[0] = y_scr[:, c:c + RET_W_].astype(BF16)
    c += RET_W_
    rg_ref[0] = y_scr[:, c:c + RET_W_]
    c += RET_W_
    put(dq_ref, c, DIFF_W_, DIFF_Q_SCALE_, qkg_ref[0:1, :])
    c += DIFF_W_
    put(dk_ref, c, DIFF_W_, 1.0, qkg_ref[1:2, :])
    c += DIFF_W_
    dv_ref[0] = y_scr[:, c:c + DIFF_W_].astype(BF16)


def _inproj(xs, mod3, mod_row, norm_g, w_in_bf, tabs, qkg, rope, tm):
    b, l, d = xs.shape
    cos, su, sd = tabs
    grid = (b, l // tm)
    tok = lambda w: pl.BlockSpec((1, tm, w), lambda i, j: (i, j, 0))
    tab = pl.BlockSpec((tm, LANES_), lambda i, j: (j, 0))
    full = lambda a: pl.BlockSpec(a.shape, lambda i, j: (0,) * a.ndim)
    out_shapes = (
        jax.ShapeDtypeStruct((b, l, RET_QK_), BF16),
        jax.ShapeDtypeStruct((b, l, RET_QK_), BF16),
        jax.ShapeDtypeStruct((b, l, RET_W_), BF16),
        jax.ShapeDtypeStruct((b, l, RET_W_), F32),
        jax.ShapeDtypeStruct((b, l, DIFF_W_), BF16),
        jax.ShapeDtypeStruct((b, l, DIFF_W_), BF16),
        jax.ShapeDtypeStruct((b, l, DIFF_W_), BF16),
    )
    return pl.pallas_call(
        functools.partial(_inproj_kernel, rope=rope),
        grid=grid,
        in_specs=[
            tok(d),
            pl.BlockSpec((1, 1, d), lambda i, j: (mod_row(i), 0, 0)),
            pl.BlockSpec((1, 1, d), lambda i, j: (mod_row(i), 0, 1)),
            full(norm_g),
            full(w_in_bf),
            tab, tab, tab,
            full(qkg),
        ],
        out_specs=(tok(RET_QK_), tok(RET_QK_), tok(RET_W_), tok(RET_W_),
                   tok(DIFF_W_), tok(DIFF_W_), tok(DIFF_W_)),
        out_shape=out_shapes,
        scratch_shapes=[pltpu.VMEM((tm, IN_COLS_), F32)],
        compiler_params=_params(("parallel", "parallel")),
        name="inproj_rope" if rope else "inproj_ctx",
    )(xs, mod3, mod3, norm_g, w_in_bf, cos, su, sd, qkg)


def _log_sigmoid(x):
    return jnp.minimum(x, 0.0) - jnp.log(1.0 + jnp.exp(-jnp.abs(x)))


def _retention_kernel(dl_ref, q_ref, k_ref, v_ref, rg_ref, ck_ref, cv_ref, g_ref, o_ref,
                      lf_scr, lb_scr, rf_scr, rb_scr, *, seq, ctx_len):
    c = RET_CHUNK_
    n_chunks = seq // c
    head = pl.program_id(1)
    lane = lax.broadcasted_iota(jnp.int32, (1, LANES_), 1)
    head_lanes = (lane // RET_DK_) == (head % 2)

    dl = dl_ref[0]
    lg_f = _log_sigmoid(dl[0:1, :])
    lg_b = _log_sigmoid(dl[1:2, :])

    row = lax.broadcasted_iota(jnp.int32, (c, LANES_), 0).astype(F32)
    col = lax.broadcasted_iota(jnp.int32, (c, LANES_), 1).astype(F32)
    kw_f = jnp.exp((c - 1.0 - row) * lg_f)
    kw_b = jnp.exp(row * lg_b)
    qw_f = jnp.exp((row + 1.0) * lg_f)
    qw_b = jnp.exp((c - row) * lg_b)
    dist = row - col
    decay = jnp.where(dist >= 0.0,
                      jnp.exp(jnp.maximum(dist, 0.0) * lg_f),
                      jnp.exp(jnp.maximum(-dist, 0.0) * lg_b))
    chunk_f = jnp.exp(float(c) * lg_f)
    chunk_b = jnp.exp(float(c) * lg_b)

    crow = lax.broadcasted_iota(jnp.int32, (ctx_len, LANES_), 0).astype(F32)
    ck = ck_ref[0].astype(F32)
    cv = cv_ref[0]
    s_f = _dot((ck * jnp.exp((ctx_len - 1.0 - crow) * lg_f)).T.astype(BF16), cv)
    s_b = _dot((ck * jnp.exp(crow * lg_b)).T.astype(BF16), cv)

    def local_states(n, carry):
        sl = pl.ds(pl.multiple_of(n * c, c), c)
        kn = k_ref[0, sl, :].astype(F32)
        vn = v_ref[0, sl, :]
        lf_scr[n] = _dot((kn * kw_f).T.astype(BF16), vn)
        lb_scr[n] = _dot((kn * kw_b).T.astype(BF16), vn)
        return carry

    lax.fori_loop(0, n_chunks, local_states, 0, unroll=True)

    def scan_f(n, r):
        rf_scr[n] = r
        return chunk_f * r + lf_scr[n]

    lax.fori_loop(0, n_chunks, scan_f, s_f)

    def scan_b(m, r):
        n = n_chunks - 1 - m
        rb_scr[n] = r
        return chunk_b * r + lb_scr[n]

    lax.fori_loop(0, n_chunks, scan_b, s_b)

    gain = g_ref[0]

    def outputs(n, carry):
        sl = pl.ds(pl.multiple_of(n * c, c), c)
        qn = jnp.where(head_lanes, q_ref[0, sl, :], jnp.zeros((), BF16))
        scores = _dot_nt(qn, k_ref[0, sl, :])
        y = _dot((scores * decay).astype(BF16), v_ref[0, sl, :])
        y = y + qw_f * _dot(qn, rf_scr[n].astype(BF16)) + qw_b * _dot(qn, rb_scr[n].astype(BF16))
        ms = jnp.mean(y * y, axis=-1, keepdims=True)
        yn = y * lax.rsqrt(ms + NORM_EPS_) * gain
        rg = rg_ref[0, sl, :]
        o_ref[0, sl, :] = (yn * (rg * _sigmoid(rg))).astype(o_ref.dtype)
        return carry

    lax.fori_loop(0, n_chunks, outputs, 0, unroll=True)


def _retention(rq, rk, rv, rg, crk, crv, decay_b, norm_g3):
    b, l, _ = rq.shape
    ctx_len = crk.shape[1]
    n_chunks = l // RET_CHUNK_
    pair = lambda length: pl.BlockSpec((1, length, LANES_), lambda i, h: (i, 0, h // 2))
    own = lambda length: pl.BlockSpec((1, length, LANES_), lambda i, h: (i, 0, h))
    state = pltpu.VMEM((n_chunks, LANES_, RET_DV_), F32)
    return pl.pallas_call(
        functools.partial(_retention_kernel, seq=l, ctx_len=ctx_len),
        grid=(b, RET_HEADS_),
        in_specs=[
            pl.BlockSpec((1, 2, LANES_), lambda i, h: (h, 0, 0)),
            pair(l), pair(l), own(l), own(l), pair(ctx_len), own(ctx_len),
            pl.BlockSpec((1, 1, RET_DV_), lambda i, h: (h, 0, 0)),
        ],
        out_specs=own(l),
        out_shape=jax.ShapeDtypeStruct((b, l, RET_W_), BF16),
        scratch_shapes=[state, state, state, state],
        compiler_params=_params(("parallel", "parallel")),
        name="retention",
    )(decay_b, rq, rk, rv, rg, crk, crv, norm_g3)


def _diffattn_kernel(lam_ref, q_ref, k_ref, v_ref, ck_ref, cv_ref, g_ref, o_ref):
    q = q_ref[0]
    lane = lax.broadcasted_iota(jnp.int32, (1, LANES_), 1)
    first = lane < DIFF_D_
    zero = jnp.zeros((), BF16)
    q1 = jnp.where(first, q, zero)
    q2 = jnp.where(first, zero, q)
    lp = lam_ref[...]
    lam = (jnp.exp(jnp.sum(lp[0:1] * lp[1:2], axis=-1, keepdims=True))
           - jnp.exp(jnp.sum(lp[2:3] * lp[3:4], axis=-1, keepdims=True)) + LAM_INIT_)

    def attend(qh):
        s_c = _dot_nt(qh, ck_ref[0])
        m_c = jnp.max(s_c, axis=-1, keepdims=True)
        p_c = jnp.exp2(s_c - m_c)
        l_c = jnp.sum(p_c, axis=-1, keepdims=True)
        a_c = _dot(p_c.astype(BF16), cv_ref[0])
        s_l = _dot_nt(qh, k_ref[0])
        m = jnp.maximum(m_c, jnp.max(s_l, axis=-1, keepdims=True))
        alpha = jnp.exp2(m_c - m)
        p_l = jnp.exp2(s_l - m)
        denom = alpha * l_c + jnp.sum(p_l, axis=-1, keepdims=True)
        return (alpha * a_c + _dot(p_l.astype(BF16), v_ref[0])) / denom

    y = attend(q1) - lam * attend(q2)
    ms = jnp.mean(y * y, axis=-1, keepdims=True)
    yn = y * lax.rsqrt(ms + NORM_EPS_) * g_ref[0]
    o_ref[0] = (yn * (1.0 - LAM_INIT_)).astype(o_ref.dtype)


def _diffattn(dq, dk, dv, cdk, cdv, lam_params, norm_g3, tq):
    b, l, _ = dq.shape
    ctx_len = cdk.shape[1]
    kv = lambda length: pl.BlockSpec((1, length, LANES_), lambda i, h, j: (i, 0, h))
    qo = pl.BlockSpec((1, tq, LANES_), lambda i, h, j: (i, j, h))
    return pl.pallas_call(
        _diffattn_kernel,
        grid=(b, DIFF_HEADS_, l // tq),
        in_specs=[
            pl.BlockSpec(lam_params.shape, lambda i, h, j: (0, 0)),
            qo, kv(l), kv(l), kv(ctx_len), kv(ctx_len),
            pl.BlockSpec((1, 1, DIFF_DV_), lambda i, h, j: (h, 0, 0)),
        ],
        out_specs=qo,
        out_shape=jax.ShapeDtypeStruct((b, l, DIFF_W_), BF16),
        compiler_params=_params(("parallel", "parallel", "parallel")),
        name="diffattn",
    )(lam_params, dq, dk, dv, cdk, cdv, norm_g3)


def _mixout_kernel(ret_ref, dif_ref, x_ref, gate_ref, sh_ref, sc_ref, g_ref, woa_ref, wob_ref,
                   wqt_ref, keys_ref, x1_ref, ht_ref, st_ref):
    o = _dot(ret_ref[0], woa_ref[...]) + _dot(dif_ref[0], wob_ref[...])
    x1 = x_ref[0] + gate_ref[0] * o
    x1_ref[0] = x1
    ms = jnp.mean(x1 * x1, axis=-1, keepdims=True)
    h2 = x1 * lax.rsqrt(ms + NORM_EPS_) * g_ref[...]
    h2 = h2 * (1.0 + sc_ref[0]) + sh_ref[0]
    ht = h2.T.astype(BF16)
    ht_ref[...] = ht
    qt = _dot(wqt_ref[...], ht).astype(BF16)
    for hp in range(2 * PEER_HEADS_):
        st_ref[hp] = _dot(keys_ref[hp], qt[hp * PEER_HALF_:(hp + 1) * PEER_HALF_, :])


def _mixout(ret, dif, x, mod3, norm_g, wo_a, wo_b, wq_t, keys, tm):
    b, l, d = x.shape
    t = b * l
    per = l // tm
    tok = lambda w: pl.BlockSpec((1, tm, w), lambda i, j: (i, j, 0))
    modrow = lambda k: pl.BlockSpec((1, 1, d), lambda i, j: (i, 0, k))
    full = lambda a: pl.BlockSpec(a.shape, lambda i, j: (0,) * a.ndim)
    n_hp = 2 * PEER_HEADS_
    return pl.pallas_call(
        _mixout_kernel,
        grid=(b, per),
        in_specs=[tok(RET_W_), tok(DIFF_W_), tok(d), modrow(2), modrow(3), modrow(4),
                  full(norm_g), full(wo_a), full(wo_b), full(wq_t), full(keys)],
        out_specs=(
            tok(d),
            pl.BlockSpec((d, tm), lambda i, j: (0, i * per + j)),
            pl.BlockSpec((n_hp, PEER_NKEYS_, tm), lambda i, j: (0, 0, i * per + j)),
        ),
        out_shape=(
            jax.ShapeDtypeStruct((b, l, d), F32),
            jax.ShapeDtypeStruct((d, t), BF16),
            jax.ShapeDtypeStruct((n_hp, PEER_NKEYS_, t), F32),
        ),
        compiler_params=_params(("parallel", "parallel")),
        name="mix_out",
    )(ret, dif, x, mod3, mod3, mod3, norm_g, wo_a, wo_b, wq_t, keys)


def _sort16_pairs():
    def merge(lo, hi, r):
        step = r * 2
        if step < hi - lo:
            yield from merge(lo, hi, step)
            yield from merge(lo + r, hi, step)
            for i in range(lo + r, hi - r, step):
                yield (i, i + r)
        else:
            yield (lo, lo + r)

    def sort(lo, hi):
        if hi - lo >= 1:
            mid = lo + (hi - lo) // 2
            yield from sort(lo, mid)
            yield from sort(mid + 1, hi)
            yield from merge(lo, hi, 1)

    return tuple(sort(0, 15))


SORT16_PAIRS_ = _sort16_pairs()


def _bitonic_merge_desc(c):
    c = list(c)
    d = len(c) // 2
    while d >= 1:
        for k in range(len(c)):
            if k & d == 0:
                hi = jnp.maximum(c[k], c[k + d])
                lo = jnp.minimum(c[k], c[k + d])
                c[k], c[k + d] = hi, lo
        d //= 2
    return c


def _merge_top(a, b):
    n = len(a)
    return _bitonic_merge_desc([jnp.maximum(a[k], b[n - 1 - k]) for k in range(n)])


def _merge_sublanes(a):
    for shift in (4, 2, 1):
        a = _merge_top(a, [pltpu.roll(v, shift, 0) for v in a])
    return a


def _top16_rows(x):
    n = PEER_NKEYS_ // SUBLANES_
    a = [x[v * SUBLANES_:(v + 1) * SUBLANES_, :] for v in range(n)]
    for i, j in SORT16_PAIRS_:
        hi = jnp.maximum(a[i], a[j])
        lo = jnp.minimum(a[i], a[j])
        a[i], a[j] = hi, lo
    return _merge_sublanes(a)


def _route_group(s1, s2):
    w = s1.shape[-1]
    k = PEER_TOPK_
    v1 = _top16_rows(s1)
    v2 = _top16_rows(s2)
    sub = lax.broadcasted_iota(jnp.int32, (SUBLANES_, w), 0)
    inf = jnp.float32(jnp.inf)

    def pack(vals):
        out = vals[SUBLANES_ - 1]
        for a in range(SUBLANES_ - 2, -1, -1):
            out = jnp.where(sub == a, vals[a], out)
        return out

    v1_lo = pack(v1[:SUBLANES_])
    v1_hi = pack(v1[SUBLANES_:])
    cand = [jnp.where(sub < k // (b + 1), v1_lo + v2[b], -inf) for b in range(k)]
    cand_hi = v1_hi + v2[0]
    merged = list(cand)
    merged[k - 1] = jnp.maximum(merged[k - 1], cand_hi)
    best = _merge_sublanes(_bitonic_merge_desc(merged))
    top = best[0]
    cut = best[k - 1]
    z = jnp.exp(best[0] - top)
    for b in range(1, k):
        z = z + jnp.exp(best[b] - top)
    rz = (2.0 ** -0.5) / z

    n_lo = jnp.zeros((SUBLANES_, w), F32)
    for b in range(k):
        n_lo = n_lo + jnp.where(cand[b] >= cut, 1.0, 0.0)
    n_hi = jnp.where(cand_hi >= cut, 1.0, 0.0)
    n_rank = ([jnp.broadcast_to(n_lo[a:a + 1, :], (SUBLANES_, w)) for a in range(SUBLANES_)]
              + [jnp.broadcast_to(n_hi[a:a + 1, :], (SUBLANES_, w)) for a in range(SUBLANES_)])

    rank2, e2, count, c1 = [], [], [], []
    for v in range(PEER_NKEYS_ // SUBLANES_):
        rows = slice(v * SUBLANES_, (v + 1) * SUBLANES_)
        x1 = s1[rows, :]
        x2 = s2[rows, :]
        cnt = jnp.zeros((SUBLANES_, w), F32)
        for a in range(k):
            cnt = jnp.where(x1 == v1[a], n_rank[a], cnt)
        rk = jnp.full((SUBLANES_, w), float(k), F32)
        for b in range(k - 1, -1, -1):
            rk = jnp.where(x2 >= v2[b], float(b), rk)
        count.append(cnt)
        c1.append(jnp.exp(x1 - v1[0]) * rz)
        rank2.append(rk)
        e2.append(jnp.exp(x2 - v2[0]))
    cat = lambda parts: jnp.concatenate(parts, axis=0)
    return cat(rank2), cat(e2), cat(count), cat(c1)


def _route_kernel(s_ref, r2_ref, e2_ref, n_ref, c1_ref, *, group):
    tr = s_ref.shape[-1]
    packed = (PEER_NKEYS_ // (2 * SUBLANES_), 2 * SUBLANES_, group)

    def body(g, carry):
        cols = pl.ds(pl.multiple_of(g * group, group), group)
        rank2, e2, count, c1 = _route_group(s_ref[0, 0, :, cols], s_ref[0, 1, :, cols])
        r2_ref[0, :, :, cols] = rank2.astype(BF16).reshape(packed)
        e2_ref[0, :, :, cols] = e2.astype(BF16).reshape(packed)
        n_ref[0, :, cols] = count
        c1_ref[0, :, cols] = c1
        return carry

    lax.fori_loop(0, tr // group, body, 0)


def _route(st4, tr, group):
    heads, _, nk, t = st4.shape
    pk = (nk // (2 * SUBLANES_), 2 * SUBLANES_)
    out_pk = jax.ShapeDtypeStruct((heads,) + pk + (t,), BF16)
    out_f = jax.ShapeDtypeStruct((heads, nk, t), F32)
    spec_pk = pl.BlockSpec((1,) + pk + (tr,), lambda h, j: (h, 0, 0, j))
    spec_f = pl.BlockSpec((1, nk, tr), lambda h, j: (h, 0, j))
    return pl.pallas_call(
        functools.partial(_route_kernel, group=group),
        grid=(heads, t // tr),
        in_specs=[pl.BlockSpec((1, 2, nk, tr), lambda h, j: (h, 0, 0, j))],
        out_specs=(spec_pk, spec_pk, spec_f, spec_f),
        out_shape=(out_pk, out_pk, out_f, out_f),
        compiler_params=_params(("parallel", "parallel")),
        name="peer_route",
    )(st4)


def _peer_kernel(ht_ref, u_ref, vt_ref, r2_ref, e2_ref, n_ref, c1_ref, x1_ref, gate_ref, o_ref,
                 acc_ref, g_scr, *, chunk):
    e = pl.program_id(1)
    te = u_ref.shape[0]
    tt = ht_ref.shape[1]
    pk = 2 * SUBLANES_
    rows_per_chunk = chunk // PEER_NKEYS_
    zero = jnp.zeros((), BF16)
    n_chunks = te // chunk
    groups = PEER_NKEYS_ // pk

    def fold(pair):
        lo = 2 * pair * chunk
        g_pair = g_scr[2 * pair:2 * pair + 2].reshape(2 * chunk, tt)
        acc_ref[...] += _dot(vt_ref[:, lo:lo + 2 * chunk], g_pair)

    def weighted_activations(c, a):
        act = (a * (1.0 + lax.erf(a))).astype(BF16)
        act = act.reshape(rows_per_chunk * groups, pk, tt)
        for r in range(rows_per_chunk):
            i = c * rows_per_chunk + r
            spread = lambda ref, h: jnp.broadcast_to(ref[h, i:i + 1, :], (pk, tt)).astype(BF16)[None]
            cnt_rows = [spread(n_ref, h) for h in range(PEER_HEADS_)]
            c1_rows = [spread(c1_ref, h) for h in range(PEER_HEADS_)]
            for lg in range(tt // PACKED_LANES_):
                cols = slice(lg * PACKED_LANES_, (lg + 1) * PACKED_LANES_)
                cnts = [v[:, :, cols] for v in cnt_rows]
                c1s = [v[:, :, cols] for v in c1_rows]
                for s in range(groups):
                    w = None
                    for h in range(PEER_HEADS_):
                        term = jnp.where(r2_ref[h, s:s + 1, :, cols] < cnts[h],
                                         e2_ref[h, s:s + 1, :, cols], zero) * c1s[h]
                        w = term if w is None else w + term
                    g_scr[c, r * groups + s:r * groups + s + 1, :, cols] = (
                        w * act[r * groups + s:r * groups + s + 1, :, cols])

    @pl.when(e == 0)
    def _():
        acc_ref[...] = jnp.zeros_like(acc_ref)

    ht = ht_ref[...]
    first_dot = lambda c: _dot(u_ref[c * chunk:(c + 1) * chunk, :], ht)
    a_next = first_dot(0)
    for c in range(n_chunks):
        a = a_next
        if c + 1 < n_chunks:
            a_next = first_dot(c + 1)
        weighted_activations(c, a)
        if c % 2 == 1:
            fold(c // 2)

    @pl.when(e == pl.num_programs(1) - 1)
    def _():
        o_ref[0] = x1_ref[0] + gate_ref[0] * acc_ref[...].T


def _peer(ht, u_bf, vt_bf, rank2, e2, count, c1, x1, mod3, tt, te):
    d, t = ht.shape
    n_exp = u_bf.shape[0]
    b, l, _ = x1.shape
    per = l // tt
    rows = te // PEER_NKEYS_
    pk = (PEER_NKEYS_ // (2 * SUBLANES_), 2 * SUBLANES_)
    chunk = 2 * PEER_NKEYS_
    g_scratch = pltpu.VMEM((te // chunk, chunk // pk[1], pk[1], tt), BF16)
    return pl.pallas_call(
        functools.partial(_peer_kernel, chunk=chunk),
        grid=(t // tt, n_exp // te),
        in_specs=[
            pl.BlockSpec((d, tt), lambda i, e: (0, i)),
            pl.BlockSpec((te, d), lambda i, e: (e, 0)),
            pl.BlockSpec((d, te), lambda i, e: (0, e)),
            pl.BlockSpec((PEER_HEADS_,) + pk + (tt,), lambda i, e: (0, 0, 0, i)),
            pl.BlockSpec((PEER_HEADS_,) + pk + (tt,), lambda i, e: (0, 0, 0, i)),
            pl.BlockSpec((PEER_HEADS_, rows, tt), lambda i, e: (0, e, i)),
            pl.BlockSpec((PEER_HEADS_, rows, tt), lambda i, e: (0, e, i)),
            pl.BlockSpec((1, tt, d), lambda i, e: (i // per, i % per, 0)),
            pl.BlockSpec((1, 1, d), lambda i, e: (i // per, 0, 5)),
        ],
        out_specs=pl.BlockSpec((1, tt, d), lambda i, e: (i // per, i % per, 0)),
        out_shape=jax.ShapeDtypeStruct((b, l, d), F32),
        scratch_shapes=[pltpu.VMEM((d, tt), F32), g_scratch],
        compiler_params=_params(("parallel", "arbitrary")),
        name="peer_dense",
    )(ht, u_bf, vt_bf, rank2, e2, count, c1, x1, mod3)


def _transpose_cast_kernel(x_ref, o_ref):
    o_ref[...] = x_ref[...].T.astype(o_ref.dtype)


def _transpose_cast(x, tile):
    r, c = x.shape
    return pl.pallas_call(
        _transpose_cast_kernel,
        grid=(r // tile,),
        in_specs=[pl.BlockSpec((tile, c), lambda i: (i, 0))],
        out_specs=pl.BlockSpec((c, tile), lambda i: (0, i)),
        out_shape=jax.ShapeDtypeStruct((c, r), BF16),
        compiler_params=_params(("parallel",)),
        name="transpose_cast",
    )(x)


def _rope_tables(seq):
    quarter = RET_DK_ // 4
    freqs = ROPE_BASE_ ** (-jnp.arange(quarter, dtype=F32) / quarter)
    rows = seq // GRID_W_
    row = jnp.repeat(jnp.arange(rows, dtype=F32), GRID_W_)
    col = jnp.tile(jnp.arange(GRID_W_, dtype=F32), rows)
    ar = row[:, None] * freqs
    ac = col[:, None] * freqs
    ang = jnp.concatenate([ar, ar, ac, ac], axis=-1)
    ang = jnp.concatenate([ang, ang], axis=-1)
    cos, sin = jnp.cos(ang), jnp.sin(ang)
    first = (jnp.arange(LANES_) % 32) < 16
    return cos, jnp.where(first, -sin, 0.0), jnp.where(first, 0.0, sin)


def _tiles(seq, ctx_len, tokens):
    pick = lambda n, pref: next(c for c in pref if n % c == 0)
    return dict(
        tm_in=pick(seq, (1024, 512, 256, 128)),
        tm_ctx=pick(ctx_len, (256, 128)),
        tq=pick(seq, (2048, 1024, 512, 256, 128)),
        tm_mix=pick(seq, (1024, 512, 256, 128)),
        tr=pick(tokens, (1024, 512, 256, 128)),
        route_group=pick(tokens, (256, 128)),
        tt=pick(seq, (1024, 512, 256)),
        te=2048,
    )


def kernel(x, c, ctx, c_ctx, w_mod, b_mod, norm1_g, norm2_g, w_in, ret_decay_logit, ret_norm_g,
           diff_qk_norm_g, diff_lambda, diff_norm_g, w_out, peer_w_query, peer_sub_keys, peer_u, peer_v):
    assert w_mod.shape[0] == 1, "single-layer stack"
    b, l, d = x.shape
    ctx_len = ctx.shape[1]
    t = b * l
    tiles = _tiles(l, ctx_len, t)

    n_rows = -(-(b + 1) // SUBLANES_) * SUBLANES_
    cond = jnp.zeros((n_rows, d), F32).at[:b].set(c).at[b].set(c_ctx)
    mod = _adaln(cond, w_mod[0], b_mod[0])
    mod3 = mod.reshape(n_rows, 1, N_MOD_ * d)

    w_in_bf = w_in[0].astype(BF16)
    g1 = norm1_g[0].reshape(1, d)
    g2 = norm2_g[0].reshape(1, d)
    qkg = jnp.tile(diff_qk_norm_g[0], (1, DIFF_W_ // DIFF_D_))
    tabs = _rope_tables(l)
    rq, rk, rv, rg, dq, dk, dv = _inproj(x, mod3, lambda i: i, g1, w_in_bf, tabs, qkg, True,
                                         tiles["tm_in"])
    ctabs = tuple(jnp.zeros((ctx_len, LANES_), F32) for _ in range(3))
    _, crk, crv, _, _, cdk, cdv = _inproj(ctx, mod3, lambda i: b, g1, w_in_bf, ctabs, qkg, False,
                                          tiles["tm_ctx"])

    decay_b = jnp.broadcast_to(ret_decay_logit[0].T[:, :, None], (RET_HEADS_, 2, LANES_))
    ret = _retention(rq, rk, rv, rg, crk, crv, decay_b, ret_norm_g[0].reshape(RET_HEADS_, 1, RET_DV_))
    dif = _diffattn(dq, dk, dv, cdk, cdv, diff_lambda[0],
                    diff_norm_g[0].reshape(DIFF_HEADS_, 1, DIFF_DV_), tiles["tq"])

    wo = w_out[0].astype(BF16)
    wq_t = peer_w_query[0].T.astype(BF16)
    keys = peer_sub_keys[0].reshape(2 * PEER_HEADS_, PEER_NKEYS_, PEER_HALF_).astype(BF16)
    x1, ht, st = _mixout(ret, dif, x, mod3, g2, wo[:RET_W_], wo[RET_W_:], wq_t, keys, tiles["tm_mix"])

    st4 = st.reshape(PEER_HEADS_, 2, PEER_NKEYS_, t)
    rank2, e2, count, c1 = _route(st4, tiles["tr"], tiles["route_group"])

    u_bf = (peer_u[0] * (2.0 ** -0.5)).astype(BF16)
    vt_bf = _transpose_cast(peer_v[0], tiles["te"])
    return _peer(ht, u_bf, vt_bf, rank2, e2, count, c1, x1, mod3, tiles["tt"], tiles["te"])
```

```python
import functools
import math

import jax
import jax.numpy as jnp
from jax import lax
from jax.experimental import pallas as pl
from jax.experimental.pallas import tpu as pltpu

F32 = jnp.float32
BF16 = jnp.bfloat16

D_MODEL_ = 1024
N_MOD_ = 6
NORM_EPS_ = 1e-6
RET_HEADS_ = 4
RET_DK_ = 64
RET_DV_ = 128
RET_CHUNK_ = 128
DIFF_HEADS_ = 4
DIFF_D_ = 64
DIFF_DV_ = 128
GRID_W_ = 64
ROPE_BASE_ = 10000.0
PEER_HEADS_ = 8
PEER_NKEYS_ = 128
PEER_HALF_ = 128
PEER_TOPK_ = 16
LAM_INIT_ = 0.8 - 0.6 * math.exp(-0.3 * 0)

LANES_ = 128
SUBLANES_ = 8
PACKED_LANES_ = 2 * LANES_
VMEM_LIMIT_BYTES_ = 60 * 1024 * 1024

DIFF_Q_SCALE_ = math.log2(math.e) * DIFF_D_ ** -0.5

RET_QK_ = RET_HEADS_ * RET_DK_
RET_W_ = RET_HEADS_ * RET_DV_
DIFF_W_ = DIFF_HEADS_ * DIFF_DV_
IN_COLS_ = 2 * RET_QK_ + 2 * RET_W_ + 3 * DIFF_W_


def _dot(a, b):
    return jnp.dot(a, b, preferred_element_type=F32)


def _dot_nt(a, b):
    return lax.dot_general(a, b, (((1,), (1,)), ((), ())), preferred_element_type=F32)


def _sigmoid(x):
    return 1.0 / (1.0 + jnp.exp(-x))


def _params(sem, flags=None):
    return pltpu.CompilerParams(dimension_semantics=sem, vmem_limit_bytes=VMEM_LIMIT_BYTES_, flags=flags)


def _adaln_kernel(cond_ref, w_ref, b_ref, o_ref):
    a = cond_ref[...]
    a = a * _sigmoid(a)
    o_ref[...] = _dot(a.astype(BF16), w_ref[...].astype(BF16)) + b_ref[...]


def _adaln(cond, w_mod, b_mod):
    rows, d = cond.shape
    n = w_mod.shape[1]
    tn = d
    return pl.pallas_call(
        _adaln_kernel,
        grid=(n // tn,),
        in_specs=[
            pl.BlockSpec((rows, d), lambda j: (0, 0)),
            pl.BlockSpec((d, tn), lambda j: (0, j)),
            pl.BlockSpec((1, tn), lambda j: (0, j)),
        ],
        out_specs=pl.BlockSpec((rows, tn), lambda j: (0, j)),
        out_shape=jax.ShapeDtypeStruct((rows, n), F32),
        compiler_params=_params(("parallel",)),
        name="adaln",
    )(cond, w_mod, b_mod.reshape(1, n))


def _rope_slab(v, cos, sin_up, sin_dn):
    up = pltpu.roll(v, LANES_ - 16, 1)
    dn = pltpu.roll(v, 16, 1)
    return v * cos + up * sin_up + dn * sin_dn


def _group_rms_scale(v, eps):
    w = v.shape[-1]
    sel = (lax.broadcasted_iota(jnp.int32, (w, LANES_), 0) // DIFF_D_
           == lax.broadcasted_iota(jnp.int32, (w, LANES_), 1)).astype(BF16)
    expand = (lax.broadcasted_iota(jnp.int32, (LANES_, w), 1) // DIFF_D_
              == lax.broadcasted_iota(jnp.int32, (LANES_, w), 0)).astype(BF16)
    v2 = v * v
    hi = v2.astype(BF16)
    lo = (v2 - hi.astype(F32)).astype(BF16)
    ssum = _dot(hi, sel) + _dot(lo, sel)
    r = lax.rsqrt(ssum * (1.0 / DIFF_D_) + eps)
    rhi = r.astype(BF16)
    rlo = (r - rhi.astype(F32)).astype(BF16)
    return _dot(rhi, expand) + _dot(rlo, expand)


def _inproj_kernel(x_ref, sh_ref, sc_ref, g_ref, w_ref, cos_ref, su_ref, sd_ref, qkg_ref,
                   rq_ref, rk_ref, rv_ref, rg_ref, dq_ref, dk_ref, dv_ref, y_scr, *, rope):
    x = x_ref[0]
    ms = jnp.mean(x * x, axis=-1, keepdims=True)
    h = x * lax.rsqrt(ms + NORM_EPS_) * g_ref[...]
    h = h * (1.0 + sc_ref[0]) + sh_ref[0]
    y_scr[...] = _dot(h.astype(BF16), w_ref[...])

    cos = cos_ref[...]
    su = su_ref[...]
    sd = sd_ref[...]

    def put(dst_ref, src_col, width, scale, norm_g):
        v = y_scr[:, src_col:src_col + width]
        if norm_g is not None:
            v = v * _group_rms_scale(v, NORM_EPS_) * norm_g
        for s in range(width // LANES_):
            slab = v[:, s * LANES_:(s + 1) * LANES_]
            if rope:
                slab = _rope_slab(slab, cos, su, sd)
            if scale != 1.0:
                slab = slab * scale
            dst_ref[0, :, s * LANES_:(s + 1) * LANES_] = slab.astype(dst_ref.dtype)

    c = 0
    put(rq_ref, c, RET_QK_, 1.0, None)
    c += RET_QK_
    put(rk_ref, c, RET_QK_, RET_DK_ ** -0.5, None)
    c += RET_QK_
    rv_ref[0] = y_scr[:, c:c + RET_W_].astype(BF16)
    c += RET_W_
    rg_ref[0] = y_scr[:, c:c + RET_W_]
    c += RET_W_
    put(dq_ref, c, DIFF_W_, DIFF_Q_SCALE_, qkg_ref[0:1, :])
    c += DIFF_W_
    put(dk_ref, c, DIFF_W_, 1.0, qkg_ref[1:2, :])
    c += DIFF_W_
    dv_ref[0] = y_scr[:, c:c + DIFF_W_].astype(BF16)


def _inproj(xs, mod3, mod_row, norm_g, w_in_bf, tabs, qkg, rope, tm):
    b, l, d = xs.shape
    cos, su, sd = tabs
    grid = (b, l // tm)
    tok = lambda w: pl.BlockSpec((1, tm, w), lambda i, j: (i, j, 0))
    tab = pl.BlockSpec((tm, LANES_), lambda i, j: (j, 0))
    full = lambda a: pl.BlockSpec(a.shape, lambda i, j: (0,) * a.ndim)
    out_shapes = (
        jax.ShapeDtypeStruct((b, l, RET_QK_), BF16),
        jax.ShapeDtypeStruct((b, l, RET_QK_), BF16),
        jax.ShapeDtypeStruct((b, l, RET_W_), BF16),
        jax.ShapeDtypeStruct((b, l, RET_W_), F32),
        jax.ShapeDtypeStruct((b, l, DIFF_W_), BF16),
        jax.ShapeDtypeStruct((b, l, DIFF_W_), BF16),
        jax.ShapeDtypeStruct((b, l, DIFF_W_), BF16),
    )
    return pl.pallas_call(
        functools.partial(_inproj_kernel, rope=rope),
        grid=grid,
        in_specs=[
            tok(d),
            pl.BlockSpec((1, 1, d), lambda i, j: (mod_row(i), 0, 0)),
            pl.BlockSpec((1, 1, d), lambda i, j: (mod_row(i), 0, 1)),
            full(norm_g),
            full(w_in_bf),
            tab, tab, tab,
            full(qkg),
        ],
        out_specs=(tok(RET_QK_), tok(RET_QK_), tok(RET_W_), tok(RET_W_),
                   tok(DIFF_W_), tok(DIFF_W_), tok(DIFF_W_)),
        out_shape=out_shapes,
        scratch_shapes=[pltpu.VMEM((tm, IN_COLS_), F32)],
        compiler_params=_params(("parallel", "parallel")),
        name="inproj_rope" if rope else "inproj_ctx",
    )(xs, mod3, mod3, norm_g, w_in_bf, cos, su, sd, qkg)


def _log_sigmoid(x):
    return jnp.minimum(x, 0.0) - jnp.log(1.0 + jnp.exp(-jnp.abs(x)))


def _retention_kernel(dl_ref, q_ref, k_ref, v_ref, rg_ref, ck_ref, cv_ref, g_ref, o_ref,
                      lf_scr, lb_scr, rf_scr, rb_scr, *, seq, ctx_len):
    c = RET_CHUNK_
    n_chunks = seq // c
    head = pl.program_id(1)
    lane = lax.broadcasted_iota(jnp.int32, (1, LANES_), 1)
    head_lanes = (lane // RET_DK_) == (head % 2)

    dl = dl_ref[0]
    lg_f = _log_sigmoid(dl[0:1, :])
    lg_b = _log_sigmoid(dl[1:2, :])

    row = lax.broadcasted_iota(jnp.int32, (c, LANES_), 0).astype(F32)
    col = lax.broadcasted_iota(jnp.int32, (c, LANES_), 1).astype(F32)
    kw_f = jnp.exp((c - 1.0 - row) * lg_f)
    kw_b = jnp.exp(row * lg_b)
    qw_f = jnp.exp((row + 1.0) * lg_f)
    qw_b = jnp.exp((c - row) * lg_b)
    dist = row - col
    decay = jnp.where(dist >= 0.0,
                      jnp.exp(jnp.maximum(dist, 0.0) * lg_f),
                      jnp.exp(jnp.maximum(-dist, 0.0) * lg_b))
    chunk_f = jnp.exp(float(c) * lg_f)
    chunk_b = jnp.exp(float(c) * lg_b)

    crow = lax.broadcasted_iota(jnp.int32, (ctx_len, LANES_), 0).astype(F32)
    ck = ck_ref[0].astype(F32)
    cv = cv_ref[0]
    s_f = _dot((ck * jnp.exp((ctx_len - 1.0 - crow) * lg_f)).T.astype(BF16), cv)
    s_b = _dot((ck * jnp.exp(crow * lg_b)).T.astype(BF16), cv)

    def local_states(n, carry):
        sl = pl.ds(pl.multiple_of(n * c, c), c)
        kn = k_ref[0, sl, :].astype(F32)
        vn = v_ref[0, sl, :]
        lf_scr[n] = _dot((kn * kw_f).T.astype(BF16), vn)
        lb_scr[n] = _dot((kn * kw_b).T.astype(BF16), vn)
        return carry

    lax.fori_loop(0, n_chunks, local_states, 0, unroll=True)

    def scan_f(n, r):
        rf_scr[n] = r
        return chunk_f * r + lf_scr[n]

    lax.fori_loop(0, n_chunks, scan_f, s_f)

    def scan_b(m, r):
        n = n_chunks - 1 - m
        rb_scr[n] = r
        return chunk_b * r + lb_scr[n]

    lax.fori_loop(0, n_chunks, scan_b, s_b)

    gain = g_ref[0]

    def outputs(n, carry):
        sl = pl.ds(pl.multiple_of(n * c, c), c)
        qn = jnp.where(head_lanes, q_ref[0, sl, :], jnp.zeros((), BF16))
        scores = _dot_nt(qn, k_ref[0, sl, :])
        y = _dot((scores * decay).astype(BF16), v_ref[0, sl, :])
        y = y + qw_f * _dot(qn, rf_scr[n].astype(BF16)) + qw_b * _dot(qn, rb_scr[n].astype(BF16))
        ms = jnp.mean(y * y, axis=-1, keepdims=True)
        yn = y * lax.rsqrt(ms + NORM_EPS_) * gain
        rg = rg_ref[0, sl, :]
        o_ref[0, sl, :] = (yn * (rg * _sigmoid(rg))).astype(o_ref.dtype)
        return carry

    lax.fori_loop(0, n_chunks, outputs, 0, unroll=True)


def _retention(rq, rk, rv, rg, crk, crv, decay_b, norm_g3):
    b, l, _ = rq.shape
    ctx_len = crk.shape[1]
    n_chunks = l // RET_CHUNK_
    pair = lambda length: pl.BlockSpec((1, length, LANES_), lambda i, h: (i, 0, h // 2))
    own = lambda length: pl.BlockSpec((1, length, LANES_), lambda i, h: (i, 0, h))
    state = pltpu.VMEM((n_chunks, LANES_, RET_DV_), F32)
    return pl.pallas_call(
        functools.partial(_retention_kernel, seq=l, ctx_len=ctx_len),
        grid=(b, RET_HEADS_),
        in_specs=[
            pl.BlockSpec((1, 2, LANES_), lambda i, h: (h, 0, 0)),
            pair(l), pair(l), own(l), own(l), pair(ctx_len), own(ctx_len),
            pl.BlockSpec((1, 1, RET_DV_), lambda i, h: (h, 0, 0)),
        ],
        out_specs=own(l),
        out_shape=jax.ShapeDtypeStruct((b, l, RET_W_), BF16),
        scratch_shapes=[state, state, state, state],
        compiler_params=_params(("parallel", "parallel")),
        name="retention",
    )(decay_b, rq, rk, rv, rg, crk, crv, norm_g3)


def _diffattn_kernel(lam_ref, q_ref, k_ref, v_ref, ck_ref, cv_ref, g_ref, o_ref):
    q = q_ref[0]
    lane = lax.broadcasted_iota(jnp.int32, (1, LANES_), 1)
    first = lane < DIFF_D_
    zero = jnp.zeros((), BF16)
    q1 = jnp.where(first, q, zero)
    q2 = jnp.where(first, zero, q)
    lp = lam_ref[...]
    lam = (jnp.exp(jnp.sum(lp[0:1] * lp[1:2], axis=-1, keepdims=True))
           - jnp.exp(jnp.sum(lp[2:3] * lp[3:4], axis=-1, keepdims=True)) + LAM_INIT_)

    def attend(qh):
        s_c = _dot_nt(qh, ck_ref[0])
        m_c = jnp.max(s_c, axis=-1, keepdims=True)
        p_c = jnp.exp2(s_c - m_c)
        l_c = jnp.sum(p_c, axis=-1, keepdims=True)
        a_c = _dot(p_c.astype(BF16), cv_ref[0])
        s_l = _dot_nt(qh, k_ref[0])
        m = jnp.maximum(m_c, jnp.max(s_l, axis=-1, keepdims=True))
        alpha = jnp.exp2(m_c - m)
        p_l = jnp.exp2(s_l - m)
        denom = alpha * l_c + jnp.sum(p_l, axis=-1, keepdims=True)
        return (alpha * a_c + _dot(p_l.astype(BF16), v_ref[0])) / denom

    y = attend(q1) - lam * attend(q2)
    ms = jnp.mean(y * y, axis=-1, keepdims=True)
    yn = y * lax.rsqrt(ms + NORM_EPS_) * g_ref[0]
    o_ref[0] = (yn * (1.0 - LAM_INIT_)).astype(o_ref.dtype)


def _diffattn(dq, dk, dv, cdk, cdv, lam_params, norm_g3, tq):
    b, l, _ = dq.shape
    ctx_len = cdk.shape[1]
    kv = lambda length: pl.BlockSpec((1, length, LANES_), lambda i, h, j: (i, 0, h))
    qo = pl.BlockSpec((1, tq, LANES_), lambda i, h, j: (i, j, h))
    return pl.pallas_call(
        _diffattn_kernel,
        grid=(b, DIFF_HEADS_, l // tq),
        in_specs=[
            pl.BlockSpec(lam_params.shape, lambda i, h, j: (0, 0)),
            qo, kv(l), kv(l), kv(ctx_len), kv(ctx_len),
            pl.BlockSpec((1, 1, DIFF_DV_), lambda i, h, j: (h, 0, 0)),
        ],
        out_specs=qo,
        out_shape=jax.ShapeDtypeStruct((b, l, DIFF_W_), BF16),
        compiler_params=_params(("parallel", "parallel", "parallel")),
        name="diffattn",
    )(lam_params, dq, dk, dv, cdk, cdv, norm_g3)


def _mixout_kernel(ret_ref, dif_ref, x_ref, gate_ref, sh_ref, sc_ref, g_ref, woa_ref, wob_ref,
                   wqt_ref, keys_ref, x1_ref, ht_ref, st_ref):
    o = _dot(ret_ref[0], woa_ref[...]) + _dot(dif_ref[0], wob_ref[...])
    x1 = x_ref[0] + gate_ref[0] * o
    x1_ref[0] = x1
    ms = jnp.mean(x1 * x1, axis=-1, keepdims=True)
    h2 = x1 * lax.rsqrt(ms + NORM_EPS_) * g_ref[...]
    h2 = h2 * (1.0 + sc_ref[0]) + sh_ref[0]
    ht = h2.T.astype(BF16)
    ht_ref[...] = ht
    qt = _dot(wqt_ref[...], ht).astype(BF16)
    for hp in range(2 * PEER_HEADS_):
        st_ref[hp] = _dot(keys_ref[hp], qt[hp * PEER_HALF_:(hp + 1) * PEER_HALF_, :])


def _mixout(ret, dif, x, mod3, norm_g, wo_a, wo_b, wq_t, keys, tm):
    b, l, d = x.shape
    t = b * l
    per = l // tm
    tok = lambda w: pl.BlockSpec((1, tm, w), lambda i, j: (i, j, 0))
    modrow = lambda k: pl.BlockSpec((1, 1, d), lambda i, j: (i, 0, k))
    full = lambda a: pl.BlockSpec(a.shape, lambda i, j: (0,) * a.ndim)
    n_hp = 2 * PEER_HEADS_
    return pl.pallas_call(
        _mixout_kernel,
        grid=(b, per),
        in_specs=[tok(RET_W_), tok(DIFF_W_), tok(d), modrow(2), modrow(3), modrow(4),
                  full(norm_g), full(wo_a), full(wo_b), full(wq_t), full(keys)],
        out_specs=(
            tok(d),
            pl.BlockSpec((d, tm), lambda i, j: (0, i * per + j)),
            pl.BlockSpec((n_hp, PEER_NKEYS_, tm), lambda i, j: (0, 0, i * per + j)),
        ),
        out_shape=(
            jax.ShapeDtypeStruct((b, l, d), F32),
            jax.ShapeDtypeStruct((d, t), BF16),
            jax.ShapeDtypeStruct((n_hp, PEER_NKEYS_, t), F32),
        ),
        compiler_params=_params(("parallel", "parallel")),
        name="mix_out",
    )(ret, dif, x, mod3, mod3, mod3, norm_g, wo_a, wo_b, wq_t, keys)


def _sort16_pairs():
    def merge(lo, hi, r):
        step = r * 2
        if step < hi - lo:
            yield from merge(lo, hi, step)
            yield from merge(lo + r, hi, step)
            for i in range(lo + r, hi - r, step):
                yield (i, i + r)
        else:
            yield (lo, lo + r)

    def sort(lo, hi):
        if hi - lo >= 1:
            mid = lo + (hi - lo) // 2
            yield from sort(lo, mid)
            yield from sort(mid + 1, hi)
            yield from merge(lo, hi, 1)

    return tuple(sort(0, 15))


SORT16_PAIRS_ = _sort16_pairs()


def _bitonic_merge_desc(c):
    c = list(c)
    d = len(c) // 2
    while d >= 1:
        for k in range(len(c)):
            if k & d == 0:
                hi = jnp.maximum(c[k], c[k + d])
                lo = jnp.minimum(c[k], c[k + d])
                c[k], c[k + d] = hi, lo
        d //= 2
    return c


def _merge_top(a, b):
    n = len(a)
    return _bitonic_merge_desc([jnp.maximum(a[k], b[n - 1 - k]) for k in range(n)])


def _merge_sublanes(a):
    for shift in (4, 2, 1):
        a = _merge_top(a, [pltpu.roll(v, shift, 0) for v in a])
    return a


def _top16_rows(x):
    n = PEER_NKEYS_ // SUBLANES_
    a = [x[v * SUBLANES_:(v + 1) * SUBLANES_, :] for v in range(n)]
    for i, j in SORT16_PAIRS_:
        hi = jnp.maximum(a[i], a[j])
        lo = jnp.minimum(a[i], a[j])
        a[i], a[j] = hi, lo
    return _merge_sublanes(a)


def _route_group(s1, s2):
    w = s1.shape[-1]
    k = PEER_TOPK_
    v1 = _top16_rows(s1)
    v2 = _top16_rows(s2)
    sub = lax.broadcasted_iota(jnp.int32, (SUBLANES_, w), 0)
    inf = jnp.float32(jnp.inf)

    def pack(vals):
        out = vals[SUBLANES_ - 1]
        for a in range(SUBLANES_ - 2, -1, -1):
            out = jnp.where(sub == a, vals[a], out)
        return out

    v1_lo = pack(v1[:SUBLANES_])
    v1_hi = pack(v1[SUBLANES_:])
    cand = [jnp.where(sub < k // (b + 1), v1_lo + v2[b], -inf) for b in range(k)]
    cand_hi = v1_hi + v2[0]
    merged = list(cand)
    merged[k - 1] = jnp.maximum(merged[k - 1], cand_hi)
    best = _merge_sublanes(_bitonic_merge_desc(merged))
    top = best[0]
    cut = best[k - 1]
    z = jnp.exp(best[0] - top)
    for b in range(1, k):
        z = z + jnp.exp(best[b] - top)
    rz = (2.0 ** -0.5) / z

    n_lo = jnp.zeros((SUBLANES_, w), F32)
    for b in range(k):
        n_lo = n_lo + jnp.where(cand[b] >= cut, 1.0, 0.0)
    n_hi = jnp.where(cand_hi >= cut, 1.0, 0.0)
    n_rank = ([jnp.broadcast_to(n_lo[a:a + 1, :], (SUBLANES_, w)) for a in range(SUBLANES_)]
              + [jnp.broadcast_to(n_hi[a:a + 1, :], (SUBLANES_, w)) for a in range(SUBLANES_)])

    rank2, e2, count, c1 = [], [], [], []
    for v in range(PEER_NKEYS_ // SUBLANES_):
        rows = slice(v * SUBLANES_, (v + 1) * SUBLANES_)
        x1 = s1[rows, :]
        x2 = s2[rows, :]
        cnt = jnp.zeros((SUBLANES_, w), F32)
        for a in range(k):
            cnt = jnp.where(x1 == v1[a], n_rank[a], cnt)
        rk = jnp.full((SUBLANES_, w), float(k), F32)
        for b in range(k - 1, -1, -1):
            rk = jnp.where(x2 >= v2[b], float(b), rk)
        count.append(cnt)
        c1.append(jnp.exp(x1 - v1[0]) * rz)
        rank2.append(rk)
        e2.append(jnp.exp(x2 - v2[0]))
    cat = lambda parts: jnp.concatenate(parts, axis=0)
    return cat(rank2), cat(e2), cat(count), cat(c1)


def _route_kernel(s_ref, r2_ref, e2_ref, n_ref, c1_ref, *, group):
    tr = s_ref.shape[-1]
    packed = (PEER_NKEYS_ // (2 * SUBLANES_), 2 * SUBLANES_, group)

    def body(g, carry):
        cols = pl.ds(pl.multiple_of(g * group, group), group)
        rank2, e2, count, c1 = _route_group(s_ref[0, 0, :, cols], s_ref[0, 1, :, cols])
        r2_ref[0, :, :, cols] = rank2.astype(BF16).reshape(packed)
        e2_ref[0, :, :, cols] = e2.astype(BF16).reshape(packed)
        n_ref[0, :, cols] = count
        c1_ref[0, :, cols] = c1
        return carry

    lax.fori_loop(0, tr // group, body, 0)


def _route(st4, tr, group):
    heads, _, nk, t = st4.shape
    pk = (nk // (2 * SUBLANES_), 2 * SUBLANES_)
    out_pk = jax.ShapeDtypeStruct((heads,) + pk + (t,), BF16)
    out_f = jax.ShapeDtypeStruct((heads, nk, t), F32)
    spec_pk = pl.BlockSpec((1,) + pk + (tr,), lambda h, j: (h, 0, 0, j))
    spec_f = pl.BlockSpec((1, nk, tr), lambda h, j: (h, 0, j))
    return pl.pallas_call(
        functools.partial(_route_kernel, group=group),
        grid=(heads, t // tr),
        in_specs=[pl.BlockSpec((1, 2, nk, tr), lambda h, j: (h, 0, 0, j))],
        out_specs=(spec_pk, spec_pk, spec_f, spec_f),
        out_shape=(out_pk, out_pk, out_f, out_f),
        compiler_params=_params(("parallel", "parallel")),
        name="peer_route",
    )(st4)


def _peer_kernel(ht_ref, u_ref, vt_ref, r2_ref, e2_ref, n_ref, c1_ref, x1_ref, gate_ref, o_ref,
                 acc_ref, g_scr, *, chunk):
    e = pl.program_id(1)
    te = u_ref.shape[0]
    tt = ht_ref.shape[1]
    pk = 2 * SUBLANES_
    rows_per_chunk = chunk // PEER_NKEYS_
    zero = jnp.zeros((), BF16)
    n_chunks = te // chunk
    groups = PEER_NKEYS_ // pk

    def fold(pair):
        lo = 2 * pair * chunk
        g_pair = g_scr[2 * pair:2 * pair + 2].reshape(2 * chunk, tt)
        acc_ref[...] += _dot(vt_ref[:, lo:lo + 2 * chunk], g_pair)

    def weighted_activations(c, a):
        act = (a * (1.0 + lax.erf(a))).astype(BF16)
        act = act.reshape(rows_per_chunk * groups, pk, tt)
        for r in range(rows_per_chunk):
            i = c * rows_per_chunk + r
            spread = lambda ref, h: jnp.broadcast_to(ref[h, i:i + 1, :], (pk, tt)).astype(BF16)[None]
            cnt_rows = [spread(n_ref, h) for h in range(PEER_HEADS_)]
            c1_rows = [spread(c1_ref, h) for h in range(PEER_HEADS_)]
            for lg in range(tt // PACKED_LANES_):
                cols = slice(lg * PACKED_LANES_, (lg + 1) * PACKED_LANES_)
                cnts = [v[:, :, cols] for v in cnt_rows]
                c1s = [v[:, :, cols] for v in c1_rows]
                for s in range(groups):
                    w = None
                    for h in range(PEER_HEADS_):
                        term = jnp.where(r2_ref[h, s:s + 1, :, cols] < cnts[h],
                                         e2_ref[h, s:s + 1, :, cols], zero) * c1s[h]
                        w = term if w is None else w + term
                    g_scr[c, r * groups + s:r * groups + s + 1, :, cols] = (
                        w * act[r * groups + s:r * groups + s + 1, :, cols])

    @pl.when(e == 0)
    def _():
        acc_ref[...] = jnp.zeros_like(acc_ref)

    ht = ht_ref[...]
    first_dot = lambda c: _dot(u_ref[c * chunk:(c + 1) * chunk, :], ht)
    a_next = first_dot(0)
    for c in range(n_chunks):
        a = a_next
        if c + 1 < n_chunks:
            a_next = first_dot(c + 1)
        weighted_activations(c, a)
        if c % 2 == 1:
            fold(c // 2)

    @pl.when(e == pl.num_programs(1) - 1)
    def _():
        o_ref[0] = x1_ref[0] + gate_ref[0] * acc_ref[...].T


def _peer(ht, u_bf, vt_bf, rank2, e2, count, c1, x1, mod3, tt, te):
    d, t = ht.shape
    n_exp = u_bf.shape[0]
    b, l, _ = x1.shape
    per = l // tt
    rows = te // PEER_NKEYS_
    pk = (PEER_NKEYS_ // (2 * SUBLANES_), 2 * SUBLANES_)
    chunk = 2 * PEER_NKEYS_
    g_scratch = pltpu.VMEM((te // chunk, chunk // pk[1], pk[1], tt), BF16)
    return pl.pallas_call(
        functools.partial(_peer_kernel, chunk=chunk),
        grid=(t // tt, n_exp // te),
        in_specs=[
            pl.BlockSpec((d, tt), lambda i, e: (0, i)),
            pl.BlockSpec((te, d), lambda i, e: (e, 0)),
            pl.BlockSpec((d, te), lambda i, e: (0, e)),
            pl.BlockSpec((PEER_HEADS_,) + pk + (tt,), lambda i, e: (0, 0, 0, i)),
            pl.BlockSpec((PEER_HEADS_,) + pk + (tt,), lambda i, e: (0, 0, 0, i)),
            pl.BlockSpec((PEER_HEADS_, rows, tt), lambda i, e: (0, e, i)),
            pl.BlockSpec((PEER_HEADS_, rows, tt), lambda i, e: (0, e, i)),
            pl.BlockSpec((1, tt, d), lambda i, e: (i // per, i % per, 0)),
            pl.BlockSpec((1, 1, d), lambda i, e: (i // per, 0, 5)),
        ],
        out_specs=pl.BlockSpec((1, tt, d), lambda i, e: (i // per, i % per, 0)),
        out_shape=jax.ShapeDtypeStruct((b, l, d), F32),
        scratch_shapes=[pltpu.VMEM((d, tt), F32), g_scratch],
        compiler_params=_params(("parallel", "arbitrary")),
        name="peer_dense",
    )(ht, u_bf, vt_bf, rank2, e2, count, c1, x1, mod3)


def _transpose_cast_kernel(x_ref, o_ref):
    o_ref[...] = x_ref[...].T.astype(o_ref.dtype)


def _transpose_cast(x, tile):
    r, c = x.shape
    return pl.pallas_call(
        _transpose_cast_kernel,
        grid=(r // tile,),
        in_specs=[pl.BlockSpec((tile, c), lambda i: (i, 0))],
        out_specs=pl.BlockSpec((c, tile), lambda i: (0, i)),
        out_shape=jax.ShapeDtypeStruct((c, r), BF16),
        compiler_params=_params(("parallel",)),
        name="transpose_cast",
    )(x)


def _rope_tables(seq):
    quarter = RET_DK_ // 4
    freqs = ROPE_BASE_ ** (-jnp.arange(quarter, dtype=F32) / quarter)
    rows = seq // GRID_W_
    row = jnp.repeat(jnp.arange(rows, dtype=F32), GRID_W_)
    col = jnp.tile(jnp.arange(GRID_W_, dtype=F32), rows)
    ar = row[:, None] * freqs
    ac = col[:, None] * freqs
    ang = jnp.concatenate([ar, ar, ac, ac], axis=-1)
    ang = jnp.concatenate([ang, ang], axis=-1)
    cos, sin = jnp.cos(ang), jnp.sin(ang)
    first = (jnp.arange(LANES_) % 32) < 16
    return cos, jnp.where(first, -sin, 0.0), jnp.where(first, 0.0, sin)


def _tiles(seq, ctx_len, tokens):
    pick = lambda n, pref: next(c for c in pref if n % c == 0)
    return dict(
        tm_in=pick(seq, (1024, 512, 256, 128)),
        tm_ctx=pick(ctx_len, (256, 128)),
        tq=pick(seq, (2048, 1024, 512, 256, 128)),
        tm_mix=pick(seq, (1024, 512, 256, 128)),
        tr=pick(tokens, (1024, 512, 256, 128)),
        route_group=pick(tokens, (256, 128)),
        tt=pick(seq, (1024, 512, 256)),
        te=1024,
    )


def kernel(x, c, ctx, c_ctx, w_mod, b_mod, norm1_g, norm2_g, w_in, ret_decay_logit, ret_norm_g,
           diff_qk_norm_g, diff_lambda, diff_norm_g, w_out, peer_w_query, peer_sub_keys, peer_u, peer_v):
    assert w_mod.shape[0] == 1, "single-layer stack"
    b, l, d = x.shape
    ctx_len = ctx.shape[1]
    t = b * l
    tiles = _tiles(l, ctx_len, t)

    n_rows = -(-(b + 1) // SUBLANES_) * SUBLANES_
    cond = jnp.zeros((n_rows, d), F32).at[:b].set(c).at[b].set(c_ctx)
    mod = _adaln(cond, w_mod[0], b_mod[0])
    mod3 = mod.reshape(n_rows, 1, N_MOD_ * d)

    w_in_bf = w_in[0].astype(BF16)
    g1 = norm1_g[0].reshape(1, d)
    g2 = norm2_g[0].reshape(1, d)
    qkg = jnp.tile(diff_qk_norm_g[0], (1, DIFF_W_ // DIFF_D_))
    tabs = _rope_tables(l)
    rq, rk, rv, rg, dq, dk, dv = _inproj(x, mod3, lambda i: i, g1, w_in_bf, tabs, qkg, True,
                                         tiles["tm_in"])
    ctabs = tuple(jnp.zeros((ctx_len, LANES_), F32) for _ in range(3))
    _, crk, crv, _, _, cdk, cdv = _inproj(ctx, mod3, lambda i: b, g1, w_in_bf, ctabs, qkg, False,
                                          tiles["tm_ctx"])

    decay_b = jnp.broadcast_to(ret_decay_logit[0].T[:, :, None], (RET_HEADS_, 2, LANES_))
    ret = _retention(rq, rk, rv, rg, crk, crv, decay_b, ret_norm_g[0].reshape(RET_HEADS_, 1, RET_DV_))
    dif = _diffattn(dq, dk, dv, cdk, cdv, diff_lambda[0],
                    diff_norm_g[0].reshape(DIFF_HEADS_, 1, DIFF_DV_), tiles["tq"])

    wo = w_out[0].astype(BF16)
    wq_t = peer_w_query[0].T.astype(BF16)
    keys = peer_sub_keys[0].reshape(2 * PEER_HEADS_, PEER_NKEYS_, PEER_HALF_).astype(BF16)
    x1, ht, st = _mixout(ret, dif, x, mod3, g2, wo[:RET_W_], wo[RET_W_:], wq_t, keys, tiles["tm_mix"])

    st4 = st.reshape(PEER_HEADS_, 2, PEER_NKEYS_, t)
    rank2, e2, count, c1 = _route(st4, tiles["tr"], tiles["route_group"])

    u_bf = (peer_u[0] * (2.0 ** -0.5)).astype(BF16)
    vt_bf = _transpose_cast(peer_v[0], tiles["te"])
    return _peer(ht, u_bf, vt_bf, rank2, e2, count, c1, x1, mod3, tiles["tt"], tiles["te"])
```

```python
import functools
import math

import jax
import jax.numpy as jnp
from jax import lax
from jax.experimental import pallas as pl
from jax.experimental.pallas import tpu as pltpu

F32 = jnp.float32
BF16 = jnp.bfloat16

D_MODEL_ = 1024
N_MOD_ = 6
NORM_EPS_ = 1e-6
RET_HEADS_ = 4
RET_DK_ = 64
RET_DV_ = 128
RET_CHUNK_ = 128
DIFF_HEADS_ = 4
DIFF_D_ = 64
DIFF_DV_ = 128
GRID_W_ = 64
ROPE_BASE_ = 10000.0
PEER_HEADS_ = 8
PEER_NKEYS_ = 128
PEER_HALF_ = 128
PEER_TOPK_ = 16
LAM_INIT_ = 0.8 - 0.6 * math.exp(-0.3 * 0)

LANES_ = 128
SUBLANES_ = 8
PACKED_LANES_ = 2 * LANES_
VMEM_LIMIT_BYTES_ = 60 * 1024 * 1024

DIFF_Q_SCALE_ = math.log2(math.e) * DIFF_D_ ** -0.5

RET_QK_ = RET_HEADS_ * RET_DK_
RET_W_ = RET_HEADS_ * RET_DV_
DIFF_W_ = DIFF_HEADS_ * DIFF_DV_
IN_COLS_ = 2 * RET_QK_ + 2 * RET_W_ + 3 * DIFF_W_


def _dot(a, b):
    return jnp.dot(a, b, preferred_element_type=F32)


def _dot_nt(a, b):
    return lax.dot_general(a, b, (((1,), (1,)), ((), ())), preferred_element_type=F32)


def _sigmoid(x):
    return 1.0 / (1.0 + jnp.exp(-x))


def _params(sem, flags=None):
    return pltpu.CompilerParams(dimension_semantics=sem, vmem_limit_bytes=VMEM_LIMIT_BYTES_, flags=flags)


def _adaln_kernel(cond_ref, w_ref, b_ref, o_ref):
    a = cond_ref[...]
    a = a * _sigmoid(a)
    o_ref[...] = _dot(a.astype(BF16), w_ref[...].astype(BF16)) + b_ref[...]


def _adaln(cond, w_mod, b_mod):
    rows, d = cond.shape
    n = w_mod.shape[1]
    tn = d
    return pl.pallas_call(
        _adaln_kernel,
        grid=(n // tn,),
        in_specs=[
            pl.BlockSpec((rows, d), lambda j: (0, 0)),
            pl.BlockSpec((d, tn), lambda j: (0, j)),
            pl.BlockSpec((1, tn), lambda j: (0, j)),
        ],
        out_specs=pl.BlockSpec((rows, tn), lambda j: (0, j)),
        out_shape=jax.ShapeDtypeStruct((rows, n), F32),
        compiler_params=_params(("parallel",)),
        name="adaln",
    )(cond, w_mod, b_mod.reshape(1, n))


def _rope_slab(v, cos, sin_up, sin_dn):
    up = pltpu.roll(v, LANES_ - 16, 1)
    dn = pltpu.roll(v, 16, 1)
    return v * cos + up * sin_up + dn * sin_dn


def _group_rms_scale(v, eps):
    w = v.shape[-1]
    sel = (lax.broadcasted_iota(jnp.int32, (w, LANES_), 0) // DIFF_D_
           == lax.broadcasted_iota(jnp.int32, (w, LANES_), 1)).astype(BF16)
    expand = (lax.broadcasted_iota(jnp.int32, (LANES_, w), 1) // DIFF_D_
              == lax.broadcasted_iota(jnp.int32, (LANES_, w), 0)).astype(BF16)
    v2 = v * v
    hi = v2.astype(BF16)
    lo = (v2 - hi.astype(F32)).astype(BF16)
    ssum = _dot(hi, sel) + _dot(lo, sel)
    r = lax.rsqrt(ssum * (1.0 / DIFF_D_) + eps)
    rhi = r.astype(BF16)
    rlo = (r - rhi.astype(F32)).astype(BF16)
    return _dot(rhi, expand) + _dot(rlo, expand)


def _inproj_kernel(x_ref, sh_ref, sc_ref, g_ref, w_ref, cos_ref, su_ref, sd_ref, qkg_ref,
                   rq_ref, rk_ref, rv_ref, rg_ref, dq_ref, dk_ref, dv_ref, y_scr, *, rope):
    x = x_ref[0]
    ms = jnp.mean(x * x, axis=-1, keepdims=True)
    h = x * lax.rsqrt(ms + NORM_EPS_) * g_ref[...]
    h = h * (1.0 + sc_ref[0]) + sh_ref[0]
    y_scr[...] = _dot(h.astype(BF16), w_ref[...])

    cos = cos_ref[...]
    su = su_ref[...]
    sd = sd_ref[...]

    def put(dst_ref, src_col, width, scale, norm_g):
        v = y_scr[:, src_col:src_col + width]
        if norm_g is not None:
            v = v * _group_rms_scale(v, NORM_EPS_) * norm_g
        for s in range(width // LANES_):
            slab = v[:, s * LANES_:(s + 1) * LANES_]
            if rope:
                slab = _rope_slab(slab, cos, su, sd)
            if scale != 1.0:
                slab = slab * scale
            dst_ref[0, :, s * LANES_:(s + 1) * LANES_] = slab.astype(dst_ref.dtype)

    c = 0
    put(rq_ref, c, RET_QK_, 1.0, None)
    c += RET_QK_
    put(rk_ref, c, RET_QK_, RET_DK_ ** -0.5, None)
    c += RET_QK_
    rv_ref[0] = y_scr[:, c:c + RET_W_].astype(BF16)
    c += RET_W_
    rg_ref[0] = y_scr[:, c:c + RET_W_]
    c += RET_W_
    put(dq_ref, c, DIFF_W_, DIFF_Q_SCALE_, qkg_ref[0:1, :])
    c += DIFF_W_
    put(dk_ref, c, DIFF_W_, 1.0, qkg_ref[1:2, :])
    c += DIFF_W_
    dv_ref[0] = y_scr[:, c:c + DIFF_W_].astype(BF16)


def _inproj(xs, mod3, mod_row, norm_g, w_in_bf, tabs, qkg, rope, tm):
    b, l, d = xs.shape
    cos, su, sd = tabs
    grid = (b, l // tm)
    tok = lambda w: pl.BlockSpec((1, tm, w), lambda i, j: (i, j, 0))
    tab = pl.BlockSpec((tm, LANES_), lambda i, j: (j, 0))
    full = lambda a: pl.BlockSpec(a.shape, lambda i, j: (0,) * a.ndim)
    out_shapes = (
        jax.ShapeDtypeStruct((b, l, RET_QK_), BF16),
        jax.ShapeDtypeStruct((b, l, RET_QK_), BF16),
        jax.ShapeDtypeStruct((b, l, RET_W_), BF16),
        jax.ShapeDtypeStruct((b, l, RET_W_), F32),
        jax.ShapeDtypeStruct((b, l, DIFF_W_), BF16),
        jax.ShapeDtypeStruct((b, l, DIFF_W_), BF16),
        jax.ShapeDtypeStruct((b, l, DIFF_W_), BF16),
    )
    return pl.pallas_call(
        functools.partial(_inproj_kernel, rope=rope),
        grid=grid,
        in_specs=[
            tok(d),
            pl.BlockSpec((1, 1, d), lambda i, j: (mod_row(i), 0, 0)),
            pl.BlockSpec((1, 1, d), lambda i, j: (mod_row(i), 0, 1)),
            full(norm_g),
            full(w_in_bf),
            tab, tab, tab,
            full(qkg),
        ],
        out_specs=(tok(RET_QK_), tok(RET_QK_), tok(RET_W_), tok(RET_W_),
                   tok(DIFF_W_), tok(DIFF_W_), tok(DIFF_W_)),
        out_shape=out_shapes,
        scratch_shapes=[pltpu.VMEM((tm, IN_COLS_), F32)],
        compiler_params=_params(("parallel", "parallel")),
        name="inproj_rope" if rope else "inproj_ctx",
    )(xs, mod3, mod3, norm_g, w_in_bf, cos, su, sd, qkg)


def _log_sigmoid(x):
    return jnp.minimum(x, 0.0) - jnp.log(1.0 + jnp.exp(-jnp.abs(x)))


def _retention_kernel(dl_ref, q_ref, k_ref, v_ref, rg_ref, ck_ref, cv_ref, g_ref, o_ref,
                      lf_scr, lb_scr, rf_scr, rb_scr, *, seq, ctx_len):
    c = RET_CHUNK_
    n_chunks = seq // c
    head = pl.program_id(1)
    lane = lax.broadcasted_iota(jnp.int32, (1, LANES_), 1)
    head_lanes = (lane // RET_DK_) == (head % 2)

    dl = dl_ref[0]
    lg_f = _log_sigmoid(dl[0:1, :])
    lg_b = _log_sigmoid(dl[1:2, :])

    row = lax.broadcasted_iota(jnp.int32, (c, LANES_), 0).astype(F32)
    col = lax.broadcasted_iota(jnp.int32, (c, LANES_), 1).astype(F32)
    kw_f = jnp.exp((c - 1.0 - row) * lg_f)
    kw_b = jnp.exp(row * lg_b)
    qw_f = jnp.exp((row + 1.0) * lg_f)
    qw_b = jnp.exp((c - row) * lg_b)
    dist = row - col
    decay = jnp.where(dist >= 0.0,
                      jnp.exp(jnp.maximum(dist, 0.0) * lg_f),
                      jnp.exp(jnp.maximum(-dist, 0.0) * lg_b))
    chunk_f = jnp.exp(float(c) * lg_f)
    chunk_b = jnp.exp(float(c) * lg_b)

    crow = lax.broadcasted_iota(jnp.int32, (ctx_len, LANES_), 0).astype(F32)
    ck = ck_ref[0].astype(F32)
    cv = cv_ref[0]
    s_f = _dot((ck * jnp.exp((ctx_len - 1.0 - crow) * lg_f)).T.astype(BF16), cv)
    s_b = _dot((ck * jnp.exp(crow * lg_b)).T.astype(BF16), cv)

    def local_states(n, carry):
        sl = pl.ds(pl.multiple_of(n * c, c), c)
        kn = k_ref[0, sl, :].astype(F32)
        vn = v_ref[0, sl, :]
        lf_scr[n] = _dot((kn * kw_f).T.astype(BF16), vn)
        lb_scr[n] = _dot((kn * kw_b).T.astype(BF16), vn)
        return carry

    lax.fori_loop(0, n_chunks, local_states, 0, unroll=True)

    def scan_f(n, r):
        rf_scr[n] = r
        return chunk_f * r + lf_scr[n]

    lax.fori_loop(0, n_chunks, scan_f, s_f)

    def scan_b(m, r):
        n = n_chunks - 1 - m
        rb_scr[n] = r
        return chunk_b * r + lb_scr[n]

    lax.fori_loop(0, n_chunks, scan_b, s_b)

    gain = g_ref[0]

    def outputs(n, carry):
        sl = pl.ds(pl.multiple_of(n * c, c), c)
        qn = jnp.where(head_lanes, q_ref[0, sl, :], jnp.zeros((), BF16))
        scores = _dot_nt(qn, k_ref[0, sl, :])
        y = _dot((scores * decay).astype(BF16), v_ref[0, sl, :])
        y = y + qw_f * _dot(qn, rf_scr[n].astype(BF16)) + qw_b * _dot(qn, rb_scr[n].astype(BF16))
        ms = jnp.mean(y * y, axis=-1, keepdims=True)
        yn = y * lax.rsqrt(ms + NORM_EPS_) * gain
        rg = rg_ref[0, sl, :]
        o_ref[0, sl, :] = (yn * (rg * _sigmoid(rg))).astype(o_ref.dtype)
        return carry

    lax.fori_loop(0, n_chunks, outputs, 0, unroll=True)


def _retention(rq, rk, rv, rg, crk, crv, decay_b, norm_g3):
    b, l, _ = rq.shape
    ctx_len = crk.shape[1]
    n_chunks = l // RET_CHUNK_
    pair = lambda length: pl.BlockSpec((1, length, LANES_), lambda i, h: (i, 0, h // 2))
    own = lambda length: pl.BlockSpec((1, length, LANES_), lambda i, h: (i, 0, h))
    state = pltpu.VMEM((n_chunks, LANES_, RET_DV_), F32)
    return pl.pallas_call(
        functools.partial(_retention_kernel, seq=l, ctx_len=ctx_len),
        grid=(b, RET_HEADS_),
        in_specs=[
            pl.BlockSpec((1, 2, LANES_), lambda i, h: (h, 0, 0)),
            pair(l), pair(l), own(l), own(l), pair(ctx_len), own(ctx_len),
            pl.BlockSpec((1, 1, RET_DV_), lambda i, h: (h, 0, 0)),
        ],
        out_specs=own(l),
        out_shape=jax.ShapeDtypeStruct((b, l, RET_W_), BF16),
        scratch_shapes=[state, state, state, state],
        compiler_params=_params(("parallel", "parallel")),
        name="retention",
    )(decay_b, rq, rk, rv, rg, crk, crv, norm_g3)


def _diffattn_kernel(lam_ref, q_ref, k_ref, v_ref, ck_ref, cv_ref, g_ref, o_ref):
    q = q_ref[0]
    lane = lax.broadcasted_iota(jnp.int32, (1, LANES_), 1)
    first = lane < DIFF_D_
    zero = jnp.zeros((), BF16)
    q1 = jnp.where(first, q, zero)
    q2 = jnp.where(first, zero, q)
    lp = lam_ref[...]
    lam = (jnp.exp(jnp.sum(lp[0:1] * lp[1:2], axis=-1, keepdims=True))
           - jnp.exp(jnp.sum(lp[2:3] * lp[3:4], axis=-1, keepdims=True)) + LAM_INIT_)

    def attend(qh):
        s_c = _dot_nt(qh, ck_ref[0])
        m_c = jnp.max(s_c, axis=-1, keepdims=True)
        p_c = jnp.exp2(s_c - m_c)
        l_c = jnp.sum(p_c, axis=-1, keepdims=True)
        a_c = _dot(p_c.astype(BF16), cv_ref[0])
        s_l = _dot_nt(qh, k_ref[0])
        m = jnp.maximum(m_c, jnp.max(s_l, axis=-1, keepdims=True))
        alpha = jnp.exp2(m_c - m)
        p_l = jnp.exp2(s_l - m)
        denom = alpha * l_c + jnp.sum(p_l, axis=-1, keepdims=True)
        return (alpha * a_c + _dot(p_l.astype(BF16), v_ref[0])) / denom

    y = attend(q1) - lam * attend(q2)
    ms = jnp.mean(y * y, axis=-1, keepdims=True)
    yn = y * lax.rsqrt(ms + NORM_EPS_) * g_ref[0]
    o_ref[0] = (yn * (1.0 - LAM_INIT_)).astype(o_ref.dtype)


def _diffattn(dq, dk, dv, cdk, cdv, lam_params, norm_g3, tq):
    b, l, _ = dq.shape
    ctx_len = cdk.shape[1]
    kv = lambda length: pl.BlockSpec((1, length, LANES_), lambda i, h, j: (i, 0, h))
    qo = pl.BlockSpec((1, tq, LANES_), lambda i, h, j: (i, j, h))
    return pl.pallas_call(
        _diffattn_kernel,
        grid=(b, DIFF_HEADS_, l // tq),
        in_specs=[
            pl.BlockSpec(lam_params.shape, lambda i, h, j: (0, 0)),
            qo, kv(l), kv(l), kv(ctx_len), kv(ctx_len),
            pl.BlockSpec((1, 1, DIFF_DV_), lambda i, h, j: (h, 0, 0)),
        ],
        out_specs=qo,
        out_shape=jax.ShapeDtypeStruct((b, l, DIFF_W_), BF16),
        compiler_params=_params(("parallel", "parallel", "parallel")),
        name="diffattn",
    )(lam_params, dq, dk, dv, cdk, cdv, norm_g3)


def _mixout_kernel(ret_ref, dif_ref, x_ref, gate_ref, sh_ref, sc_ref, g_ref, woa_ref, wob_ref,
                   wqt_ref, keys_ref, x1_ref, ht_ref, st_ref):
    o = _dot(ret_ref[0], woa_ref[...]) + _dot(dif_ref[0], wob_ref[...])
    x1 = x_ref[0] + gate_ref[0] * o
    x1_ref[0] = x1
    ms = jnp.mean(x1 * x1, axis=-1, keepdims=True)
    h2 = x1 * lax.rsqrt(ms + NORM_EPS_) * g_ref[...]
    h2 = h2 * (1.0 + sc_ref[0]) + sh_ref[0]
    ht = h2.T.astype(BF16)
    ht_ref[...] = ht
    qt = _dot(wqt_ref[...], ht).astype(BF16)
    for hp in range(2 * PEER_HEADS_):
        st_ref[hp] = _dot(keys_ref[hp], qt[hp * PEER_HALF_:(hp + 1) * PEER_HALF_, :])


def _mixout(ret, dif, x, mod3, norm_g, wo_a, wo_b, wq_t, keys, tm):
    b, l, d = x.shape
    t = b * l
    per = l // tm
    tok = lambda w: pl.BlockSpec((1, tm, w), lambda i, j: (i, j, 0))
    modrow = lambda k: pl.BlockSpec((1, 1, d), lambda i, j: (i, 0, k))
    full = lambda a: pl.BlockSpec(a.shape, lambda i, j: (0,) * a.ndim)
    n_hp = 2 * PEER_HEADS_
    return pl.pallas_call(
        _mixout_kernel,
        grid=(b, per),
        in_specs=[tok(RET_W_), tok(DIFF_W_), tok(d), modrow(2), modrow(3), modrow(4),
                  full(norm_g), full(wo_a), full(wo_b), full(wq_t), full(keys)],
        out_specs=(
            tok(d),
            pl.BlockSpec((d, tm), lambda i, j: (0, i * per + j)),
            pl.BlockSpec((n_hp, PEER_NKEYS_, tm), lambda i, j: (0, 0, i * per + j)),
        ),
        out_shape=(
            jax.ShapeDtypeStruct((b, l, d), F32),
            jax.ShapeDtypeStruct((d, t), BF16),
            jax.ShapeDtypeStruct((n_hp, PEER_NKEYS_, t), F32),
        ),
        compiler_params=_params(("parallel", "parallel")),
        name="mix_out",
    )(ret, dif, x, mod3, mod3, mod3, norm_g, wo_a, wo_b, wq_t, keys)


def _sort16_pairs():
    def merge(lo, hi, r):
        step = r * 2
        if step < hi - lo:
            yield from merge(lo, hi, step)
            yield from merge(lo + r, hi, step)
            for i in range(lo + r, hi - r, step):
                yield (i, i + r)
        else:
            yield (lo, lo + r)

    def sort(lo, hi):
        if hi - lo >= 1:
            mid = lo + (hi - lo) // 2
            yield from sort(lo, mid)
            yield from sort(mid + 1, hi)
            yield from merge(lo, hi, 1)

    return tuple(sort(0, 15))


SORT16_PAIRS_ = _sort16_pairs()


def _bitonic_merge_desc(c):
    c = list(c)
    d = len(c) // 2
    while d >= 1:
        for k in range(len(c)):
            if k & d == 0:
                hi = jnp.maximum(c[k], c[k + d])
                lo = jnp.minimum(c[k], c[k + d])
                c[k], c[k + d] = hi, lo
        d //= 2
    return c


def _merge_top(a, b):
    n = len(a)
    return _bitonic_merge_desc([jnp.maximum(a[k], b[n - 1 - k]) for k in range(n)])


def _merge_sublanes(a):
    for shift in (4, 2, 1):
        a = _merge_top(a, [pltpu.roll(v, shift, 0) for v in a])
    return a


def _top16_rows(x):
    n = PEER_NKEYS_ // SUBLANES_
    a = [x[v * SUBLANES_:(v + 1) * SUBLANES_, :] for v in range(n)]
    for i, j in SORT16_PAIRS_:
        hi = jnp.maximum(a[i], a[j])
        lo = jnp.minimum(a[i], a[j])
        a[i], a[j] = hi, lo
    return _merge_sublanes(a)


def _route_group(s1, s2):
    w = s1.shape[-1]
    k = PEER_TOPK_
    v1 = _top16_rows(s1)
    v2 = _top16_rows(s2)
    sub = lax.broadcasted_iota(jnp.int32, (SUBLANES_, w), 0)
    inf = jnp.float32(jnp.inf)

    def pack(vals):
        out = vals[SUBLANES_ - 1]
        for a in range(SUBLANES_ - 2, -1, -1):
            out = jnp.where(sub == a, vals[a], out)
        return out

    v1_lo = pack(v1[:SUBLANES_])
    v1_hi = pack(v1[SUBLANES_:])
    cand = [jnp.where(sub < k // (b + 1), v1_lo + v2[b], -inf) for b in range(k)]
    cand_hi = v1_hi + v2[0]
    merged = list(cand)
    merged[k - 1] = jnp.maximum(merged[k - 1], cand_hi)
    best = _merge_sublanes(_bitonic_merge_desc(merged))
    top = best[0]
    cut = best[k - 1]
    z = jnp.exp(best[0] - top)
    for b in range(1, k):
        z = z + jnp.exp(best[b] - top)
    rz = (2.0 ** -0.5) / z

    n_lo = jnp.zeros((SUBLANES_, w), F32)
    for b in range(k):
        n_lo = n_lo + jnp.where(cand[b] >= cut, 1.0, 0.0)
    n_hi = jnp.where(cand_hi >= cut, 1.0, 0.0)
    n_rank = ([jnp.broadcast_to(n_lo[a:a + 1, :], (SUBLANES_, w)) for a in range(SUBLANES_)]
              + [jnp.broadcast_to(n_hi[a:a + 1, :], (SUBLANES_, w)) for a in range(SUBLANES_)])

    rank2, e2, count, c1 = [], [], [], []
    for v in range(PEER_NKEYS_ // SUBLANES_):
        rows = slice(v * SUBLANES_, (v + 1) * SUBLANES_)
        x1 = s1[rows, :]
        x2 = s2[rows, :]
        cnt = jnp.zeros((SUBLANES_, w), F32)
        for a in range(k):
            cnt = jnp.where(x1 == v1[a], n_rank[a], cnt)
        rk = jnp.full((SUBLANES_, w), float(k), F32)
        for b in range(k - 1, -1, -1):
            rk = jnp.where(x2 >= v2[b], float(b), rk)
        count.append(cnt)
        c1.append(jnp.exp(x1 - v1[0]) * rz)
        rank2.append(rk)
        e2.append(jnp.exp(x2 - v2[0]))
    cat = lambda parts: jnp.concatenate(parts, axis=0)
    return cat(rank2), cat(e2), cat(count), cat(c1)


def _route_kernel(s_ref, r2_ref, e2_ref, n_ref, c1_ref, *, group):
    tr = s_ref.shape[-1]
    packed = (PEER_NKEYS_ // (2 * SUBLANES_), 2 * SUBLANES_, group)

    def body(g, carry):
        cols = pl.ds(pl.multiple_of(g * group, group), group)
        rank2, e2, count, c1 = _route_group(s_ref[0, 0, :, cols], s_ref[0, 1, :, cols])
        r2_ref[0, :, :, cols] = rank2.astype(BF16).reshape(packed)
        e2_ref[0, :, :, cols] = e2.astype(BF16).reshape(packed)
        n_ref[0, :, cols] = count
        c1_ref[0, :, cols] = c1
        return carry

    lax.fori_loop(0, tr // group, body, 0)


def _route(st4, tr, group):
    heads, _, nk, t = st4.shape
    pk = (nk // (2 * SUBLANES_), 2 * SUBLANES_)
    out_pk = jax.ShapeDtypeStruct((heads,) + pk + (t,), BF16)
    out_f = jax.ShapeDtypeStruct((heads, nk, t), F32)
    spec_pk = pl.BlockSpec((1,) + pk + (tr,), lambda h, j: (h, 0, 0, j))
    spec_f = pl.BlockSpec((1, nk, tr), lambda h, j: (h, 0, j))
    return pl.pallas_call(
        functools.partial(_route_kernel, group=group),
        grid=(heads, t // tr),
        in_specs=[pl.BlockSpec((1, 2, nk, tr), lambda h, j: (h, 0, 0, j))],
        out_specs=(spec_pk, spec_pk, spec_f, spec_f),
        out_shape=(out_pk, out_pk, out_f, out_f),
        compiler_params=_params(("parallel", "parallel")),
        name="peer_route",
    )(st4)


def _peer_kernel(ht_ref, u_ref, vt_ref, r2_ref, e2_ref, n_ref, c1_ref, x1_ref, gate_ref, o_ref,
                 acc_ref, g_scr, w_scr, *, chunk):
    e = pl.program_id(1)
    te = u_ref.shape[0]
    tt = ht_ref.shape[1]
    pk = 2 * SUBLANES_
    rows_per_chunk = chunk // PEER_NKEYS_
    zero = jnp.zeros((), BF16)
    n_chunks = te // chunk
    groups = PEER_NKEYS_ // pk

    def fold(pair):
        lo = 2 * pair * chunk
        g_pair = g_scr[2 * pair:2 * pair + 2].reshape(2 * chunk, tt)
        acc_ref[...] += _dot(vt_ref[:, lo:lo + 2 * chunk], g_pair)

    def build_weights(c):
        for r in range(rows_per_chunk):
            i = c * rows_per_chunk + r
            spread = lambda ref, h: jnp.broadcast_to(ref[h, i:i + 1, :], (pk, tt)).astype(BF16)[None]
            cnt_rows = [spread(n_ref, h) for h in range(PEER_HEADS_)]
            c1_rows = [spread(c1_ref, h) for h in range(PEER_HEADS_)]
            for lg in range(tt // PACKED_LANES_):
                cols = slice(lg * PACKED_LANES_, (lg + 1) * PACKED_LANES_)
                cnts = [v[:, :, cols] for v in cnt_rows]
                c1s = [v[:, :, cols] for v in c1_rows]
                for s in range(groups):
                    w = None
                    for h in range(PEER_HEADS_):
                        term = jnp.where(r2_ref[h, s:s + 1, :, cols] < cnts[h],
                                         e2_ref[h, s:s + 1, :, cols], zero) * c1s[h]
                        w = term if w is None else w + term
                    w_scr[c, r * groups + s:r * groups + s + 1, :, cols] = w

    def weighted_activations(c, a):
        act = (a * (1.0 + lax.erf(a))).astype(BF16)
        g_scr[c] = w_scr[c] * act.reshape(rows_per_chunk * groups, pk, tt)

    @pl.when(e == 0)
    def _():
        acc_ref[...] = jnp.zeros_like(acc_ref)

    ht = ht_ref[...]
    first_dot = lambda c: _dot(u_ref[c * chunk:(c + 1) * chunk, :], ht)
    a_next = first_dot(0)
    build_weights(0)
    for c in range(n_chunks):
        a = a_next
        if c + 1 < n_chunks:
            a_next = first_dot(c + 1)
            build_weights(c + 1)
        weighted_activations(c, a)
        if c % 2 == 1:
            fold(c // 2)

    @pl.when(e == pl.num_programs(1) - 1)
    def _():
        o_ref[0] = x1_ref[0] + gate_ref[0] * acc_ref[...].T


def _peer(ht, u_bf, vt_bf, rank2, e2, count, c1, x1, mod3, tt, te):
    d, t = ht.shape
    n_exp = u_bf.shape[0]
    b, l, _ = x1.shape
    per = l // tt
    rows = te // PEER_NKEYS_
    pk = (PEER_NKEYS_ // (2 * SUBLANES_), 2 * SUBLANES_)
    chunk = 2 * PEER_NKEYS_
    g_scratch = pltpu.VMEM((te // chunk, chunk // pk[1], pk[1], tt), BF16)
    return pl.pallas_call(
        functools.partial(_peer_kernel, chunk=chunk),
        grid=(t // tt, n_exp // te),
        in_specs=[
            pl.BlockSpec((d, tt), lambda i, e: (0, i)),
            pl.BlockSpec((te, d), lambda i, e: (e, 0)),
            pl.BlockSpec((d, te), lambda i, e: (0, e)),
            pl.BlockSpec((PEER_HEADS_,) + pk + (tt,), lambda i, e: (0, 0, 0, i)),
            pl.BlockSpec((PEER_HEADS_,) + pk + (tt,), lambda i, e: (0, 0, 0, i)),
            pl.BlockSpec((PEER_HEADS_, rows, tt), lambda i, e: (0, e, i)),
            pl.BlockSpec((PEER_HEADS_, rows, tt), lambda i, e: (0, e, i)),
            pl.BlockSpec((1, tt, d), lambda i, e: (i // per, i % per, 0)),
            pl.BlockSpec((1, 1, d), lambda i, e: (i // per, 0, 5)),
        ],
        out_specs=pl.BlockSpec((1, tt, d), lambda i, e: (i // per, i % per, 0)),
        out_shape=jax.ShapeDtypeStruct((b, l, d), F32),
        scratch_shapes=[pltpu.VMEM((d, tt), F32), g_scratch, g_scratch],
        compiler_params=_params(("parallel", "arbitrary")),
        name="peer_dense",
    )(ht, u_bf, vt_bf, rank2, e2, count, c1, x1, mod3)


def _transpose_cast_kernel(x_ref, o_ref):
    o_ref[...] = x_ref[...].T.astype(o_ref.dtype)


def _transpose_cast(x, tile):
    r, c = x.shape
    return pl.pallas_call(
        _transpose_cast_kernel,
        grid=(r // tile,),
        in_specs=[pl.BlockSpec((tile, c), lambda i: (i, 0))],
        out_specs=pl.BlockSpec((c, tile), lambda i: (0, i)),
        out_shape=jax.ShapeDtypeStruct((c, r), BF16),
        compiler_params=_params(("parallel",)),
        name="transpose_cast",
    )(x)


def _rope_tables(seq):
    quarter = RET_DK_ // 4
    freqs = ROPE_BASE_ ** (-jnp.arange(quarter, dtype=F32) / quarter)
    rows = seq // GRID_W_
    row = jnp.repeat(jnp.arange(rows, dtype=F32), GRID_W_)
    col = jnp.tile(jnp.arange(GRID_W_, dtype=F32), rows)
    ar = row[:, None] * freqs
    ac = col[:, None] * freqs
    ang = jnp.concatenate([ar, ar, ac, ac], axis=-1)
    ang = jnp.concatenate([ang, ang], axis=-1)
    cos, sin = jnp.cos(ang), jnp.sin(ang)
    first = (jnp.arange(LANES_) % 32) < 16
    return cos, jnp.where(first, -sin, 0.0), jnp.where(first, 0.0, sin)


def _tiles(seq, ctx_len, tokens):
    pick = lambda n, pref: next(c for c in pref if n % c == 0)
    return dict(
        tm_in=pick(seq, (1024, 512, 256, 128)),
        tm_ctx=pick(ctx_len, (256, 128)),
        tq=pick(seq, (2048, 1024, 512, 256, 128)),
        tm_mix=pick(seq, (1024, 512, 256, 128)),
        tr=pick(tokens, (1024, 512, 256, 128)),
        route_group=pick(tokens, (256, 128)),
        tt=pick(seq, (1024, 512, 256)),
        te=1024,
    )


def kernel(x, c, ctx, c_ctx, w_mod, b_mod, norm1_g, norm2_g, w_in, ret_decay_logit, ret_norm_g,
           diff_qk_norm_g, diff_lambda, diff_norm_g, w_out, peer_w_query, peer_sub_keys, peer_u, peer_v):
    assert w_mod.shape[0] == 1, "single-layer stack"
    b, l, d = x.shape
    ctx_len = ctx.shape[1]
    t = b * l
    tiles = _tiles(l, ctx_len, t)

    n_rows = -(-(b + 1) // SUBLANES_) * SUBLANES_
    cond = jnp.zeros((n_rows, d), F32).at[:b].set(c).at[b].set(c_ctx)
    mod = _adaln(cond, w_mod[0], b_mod[0])
    mod3 = mod.reshape(n_rows, 1, N_MOD_ * d)

    w_in_bf = w_in[0].astype(BF16)
    g1 = norm1_g[0].reshape(1, d)
    g2 = norm2_g[0].reshape(1, d)
    qkg = jnp.tile(diff_qk_norm_g[0], (1, DIFF_W_ // DIFF_D_))
    tabs = _rope_tables(l)
    rq, rk, rv, rg, dq, dk, dv = _inproj(x, mod3, lambda i: i, g1, w_in_bf, tabs, qkg, True,
                                         tiles["tm_in"])
    ctabs = tuple(jnp.zeros((ctx_len, LANES_), F32) for _ in range(3))
    _, crk, crv, _, _, cdk, cdv = _inproj(ctx, mod3, lambda i: b, g1, w_in_bf, ctabs, qkg, False,
                                          tiles["tm_ctx"])

    decay_b = jnp.broadcast_to(ret_decay_logit[0].T[:, :, None], (RET_HEADS_, 2, LANES_))
    ret = _retention(rq, rk, rv, rg, crk, crv, decay_b, ret_norm_g[0].reshape(RET_HEADS_, 1, RET_DV_))
    dif = _diffattn(dq, dk, dv, cdk, cdv, diff_lambda[0],
                    diff_norm_g[0].reshape(DIFF_HEADS_, 1, DIFF_DV_), tiles["tq"])

    wo = w_out[0].astype(BF16)
    wq_t = peer_w_query[0].T.astype(BF16)
    keys = peer_sub_keys[0].reshape(2 * PEER_HEADS_, PEER_NKEYS_, PEER_HALF_).astype(BF16)
    x1, ht, st = _mixout(ret, dif, x, mod3, g2, wo[:RET_W_], wo[RET_W_:], wq_t, keys, tiles["tm_mix"])

    st4 = st.reshape(PEER_HEADS_, 2, PEER_NKEYS_, t)
    rank2, e2, count, c1 = _route(st4, tiles["tr"], tiles["route_group"])

    u_bf = (peer_u[0] * (2.0 ** -0.5)).astype(BF16)
    vt_bf = _transpose_cast(peer_v[0], tiles["te"])
    return _peer(ht, u_bf, vt_bf, rank2, e2, count, c1, x1, mod3, tiles["tt"], tiles["te"])
```
